```python
import math, functools
import jax, jax.numpy as jnp
from jax import lax
import numpy as np

D_MODEL = 1024
BATCH = 1
SEQ = 16384
DEPTH = 2

GRID_W = 64
CTX_LEN = 256
HEAD_DIM = 64
N_FOURIER_GROUPS = 4
FOURIER_GROUP_DIM = 64
FOURIER_WIDTH = N_FOURIER_GROUPS * FOURIER_GROUP_DIM
N_DIFF_HEADS = 6
DIFF_QK_DIM = HEAD_DIM // 2
DIFF_V_DIM = HEAD_DIM
DIFF_WIDTH = N_DIFF_HEADS * DIFF_V_DIM
N_GQA_Q_HEADS = 6
N_GQA_KV_HEADS = 2
GQA_GROUP = N_GQA_Q_HEADS // N_GQA_KV_HEADS
GQA_WIDTH = N_GQA_Q_HEADS * HEAD_DIM
MIX_WIDTH = FOURIER_WIDTH + DIFF_WIDTH + GQA_WIDTH
IN_SPLITS = (FOURIER_WIDTH, 2 * N_DIFF_HEADS * DIFF_QK_DIM, 2 * N_DIFF_HEADS * DIFF_QK_DIM, DIFF_WIDTH, GQA_WIDTH, N_GQA_KV_HEADS * HEAD_DIM, N_GQA_KV_HEADS * HEAD_DIM)
IN_WIDTH = 2048
N_EXPERTS = 32
TOP_K = 4
EXPERT_FF = 1024
SWIGLU_ALPHA = 1.702
SWIGLU_LIMIT = 7.0
EXPERT_BLOCK = 128
Q_BLOCK = 128
ROPE_THETA = 10000.0
NORM_EPS = 1e-6
SUBLN_EPS = 1e-5
DIFF_SCALE = DIFF_QK_DIM ** -0.5
GQA_SCALE = HEAD_DIM ** -0.5

kernel_name = 'hybrid_fourier_diffattn_gqa_moe_prefix_dit'


def rms_norm(x, g, eps=NORM_EPS):
    xf = x.astype(jnp.float32)
    y = xf * lax.rsqrt(jnp.mean(xf * xf, axis=-1, keepdims=True) + eps)
    return (y * g.astype(jnp.float32)).astype(x.dtype)


def modulate(h, shift, scale):
    return h * (1 + scale) + shift


def axial_rope_tables(row, col, rot_dim):
    n = rot_dim // 4
    inv = ROPE_THETA ** (-jnp.arange(n, dtype=jnp.float32) / n)
    ang = jnp.concatenate([row[:, None] * inv, col[:, None] * inv], axis=-1)
    return jnp.cos(ang), jnp.sin(ang)


def apply_rope(x, cos, sin):
    xf = x.astype(jnp.float32)
    half = x.shape[-1] // 2
    x1, x2 = xf[..., :half], xf[..., half:]
    cs, sn = cos[None, :, None, :], sin[None, :, None, :]
    return jnp.concatenate([x1 * cs - x2 * sn, x2 * cs + x1 * sn], axis=-1).astype(x.dtype)


def diff_attention_core(q, k, v, lam, g_subln, lam_init):
    s = jnp.einsum('bqhtd,bkhtd->bhtqk', q, k).astype(jnp.float32) * DIFF_SCALE
    p = jax.nn.softmax(s, axis=-1)
    a = (p[:, :, 0] - lam * p[:, :, 1]).astype(v.dtype)
    o = jnp.einsum('bhqk,bkhe->bqhe', a, v)
    o = rms_norm(o, g_subln, eps=SUBLN_EPS) * (1.0 - lam_init)
    return o.reshape(q.shape[0], q.shape[1], -1)


def gqa_core(q, k, v):
    s = jnp.einsum('bqhgd,bkhd->bhgqk', q, k).astype(jnp.float32) * GQA_SCALE
    p = jax.nn.softmax(s, axis=-1).astype(v.dtype)
    o = jnp.einsum('bhgqk,bkhd->bqhgd', p, v)
    return o.reshape(q.shape[0], q.shape[1], -1)


def sweep_query_blocks(core, q):
    B, L = q.shape[0], q.shape[1]
    nb = L // Q_BLOCK
    qb = jnp.moveaxis(q.reshape((B, nb, Q_BLOCK) + q.shape[2:]), 1, 0)
    ob = lax.map(core, qb)
    return jnp.moveaxis(ob, 0, 1).reshape(B, L, ob.shape[-1])


def fourier_mix(u, w_fnet):
    B, L, _ = u.shape
    ug = u.reshape(B, L, N_FOURIER_GROUPS, FOURIER_GROUP_DIM).astype(jnp.float32)
    f = jnp.fft.fft2(ug, axes=(1, 3), norm='ortho').real.astype(u.dtype)
    y = jnp.einsum('blgc,gce->blge', f, w_fnet)
    return y.reshape(B, L, FOURIER_WIDTH)


def token_mixers(h, hc, need_ctx, w_in, w_out, w_fnet, lam, lam_init, g_subln, g_q, g_k, rope_d, rope_g):
    B, L, _ = h.shape
    C = hc.shape[1]
    pts = [int(v) for v in np.cumsum(IN_SPLITS)[:-1]]
    f, dq, dk, dv, gq, gk, gv = jnp.split(h @ w_in, pts, axis=-1)
    fc, dqc, dkc, dvc, gqc, gkc, gvc = jnp.split(hc @ w_in, pts, axis=-1)
    cos_d, sin_d = rope_d
    dq = apply_rope(dq.reshape(B, L, 2 * N_DIFF_HEADS, DIFF_QK_DIM), cos_d, sin_d).reshape(B, L, N_DIFF_HEADS, 2, DIFF_QK_DIM)
    dk = apply_rope(dk.reshape(B, L, 2 * N_DIFF_HEADS, DIFF_QK_DIM), cos_d, sin_d).reshape(B, L, N_DIFF_HEADS, 2, DIFF_QK_DIM)
    dv = dv.reshape(B, L, N_DIFF_HEADS, DIFF_V_DIM)
    dkc = dkc.reshape(B, C, N_DIFF_HEADS, 2, DIFF_QK_DIM)
    dvc = dvc.reshape(B, C, N_DIFF_HEADS, DIFF_V_DIM)
    dk_all = jnp.concatenate([dkc, dk], axis=1)
    dv_all = jnp.concatenate([dvc, dv], axis=1)
    d_out = sweep_query_blocks(lambda qi: diff_attention_core(qi, dk_all, dv_all, lam, g_subln, lam_init), dq)
    cos_g, sin_g = rope_g
    gq = apply_rope(rms_norm(gq.reshape(B, L, N_GQA_Q_HEADS, HEAD_DIM), g_q), cos_g, sin_g).reshape(B, L, N_GQA_KV_HEADS, GQA_GROUP, HEAD_DIM)
    gk = apply_rope(rms_norm(gk.reshape(B, L, N_GQA_KV_HEADS, HEAD_DIM), g_k), cos_g, sin_g)
    gv = gv.reshape(B, L, N_GQA_KV_HEADS, HEAD_DIM)
    gkc = rms_norm(gkc.reshape(B, C, N_GQA_KV_HEADS, HEAD_DIM), g_k)
    gvc = gvc.reshape(B, C, N_GQA_KV_HEADS, HEAD_DIM)
    gk_all = jnp.concatenate([gkc, gk], axis=1)
    gv_all = jnp.concatenate([gvc, gv], axis=1)
    g_out = sweep_query_blocks(lambda qi: gqa_core(qi, gk_all, gv_all), gq)
    f_out = fourier_mix(f, w_fnet)
    o = jnp.concatenate([f_out, d_out, g_out], axis=-1) @ w_out
    if not need_ctx:
        return o, None
    dqc = dqc.reshape(B, C, N_DIFF_HEADS, 2, DIFF_QK_DIM)
    gqc = rms_norm(gqc.reshape(B, C, N_GQA_Q_HEADS, HEAD_DIM), g_q).reshape(B, C, N_GQA_KV_HEADS, GQA_GROUP, HEAD_DIM)
    d_out_c = diff_attention_core(dqc, dkc, dvc, lam, g_subln, lam_init)
    g_out_c = gqa_core(gqc, gkc, gvc)
    f_out_c = fourier_mix(fc, w_fnet)
    oc = jnp.concatenate([f_out_c, d_out_c, g_out_c], axis=-1) @ w_out
    return o, oc


def moe_ffn(h, w_router, b_router, w_gate_up, b_gate_up, w_down, b_down):
    N, D = h.shape
    NK = N * TOP_K
    logits = (h @ w_router + b_router).astype(jnp.float32)
    top_v, top_i = lax.top_k(logits, TOP_K)
    gates = jax.nn.softmax(top_v, axis=-1)
    flat_e = top_i.reshape(-1).astype(jnp.int32)
    flat_t = jnp.arange(NK, dtype=jnp.int32) // TOP_K
    flat_g = gates.reshape(-1)
    se, order = lax.sort((flat_e, jnp.arange(NK, dtype=jnp.int32)), num_keys=1, is_stable=True)
    counts = jax.ops.segment_sum(jnp.ones_like(flat_e), flat_e, num_segments=N_EXPERTS)
    padded = (counts + EXPERT_BLOCK - 1) // EXPERT_BLOCK * EXPERT_BLOCK
    start_sorted = jnp.cumsum(counts) - counts
    ends_pad = jnp.cumsum(padded)
    start_pad = ends_pad - padded
    dest = start_pad[se] + jnp.arange(NK, dtype=jnp.int32) - start_sorted[se]
    P = (NK + EXPERT_BLOCK - 1) // EXPERT_BLOCK * EXPERT_BLOCK + N_EXPERTS * EXPERT_BLOCK
    nb = P // EXPERT_BLOCK
    buf_t = jnp.zeros((P,), jnp.int32).at[dest].set(flat_t[order])
    buf_g = jnp.zeros((P,), jnp.float32).at[dest].set(flat_g[order])
    blk_e = jnp.minimum(jnp.searchsorted(ends_pad, jnp.arange(nb, dtype=jnp.int32) * EXPERT_BLOCK, side='right'), N_EXPERTS - 1)
    xb = h[buf_t].reshape(nb, EXPERT_BLOCK, D)

    def expert_block(args):
        xi, e = args
        gu = xi @ w_gate_up[e] + b_gate_up[e]
        glu, lin = gu[:, :EXPERT_FF], gu[:, EXPERT_FF:]
        glu = jnp.minimum(glu, SWIGLU_LIMIT)
        lin = jnp.clip(lin, -SWIGLU_LIMIT, SWIGLU_LIMIT)
        act = glu * jax.nn.sigmoid(SWIGLU_ALPHA * glu) * (lin + 1)
        return act @ w_down[e] + b_down[e]

    yb = lax.map(expert_block, (xb, blk_e)).reshape(P, D)
    return jnp.zeros((N, D), h.dtype).at[buf_t].add(yb * buf_g[:, None].astype(yb.dtype))


def setup_inputs(seed: int = 0) -> dict:
    key = jax.random.key(seed)
    ks = jax.random.split(key, 26)
    D = D_MODEL

    def nrm(k, shape, scale):
        return jax.random.normal(k, shape, jnp.float32) * scale

    return {
        'x': nrm(ks[0], (BATCH, SEQ, D), 1.0),
        'c': nrm(ks[1], (BATCH, D), 1.0),
        'ctx': nrm(ks[2], (BATCH, CTX_LEN, D), 1.0),
        'c_ctx': nrm(ks[3], (D,), 1.0),
        'w_mod': nrm(ks[4], (DEPTH, D, 6 * D), 0.5 * D ** -0.5),
        'b_mod': nrm(ks[5], (DEPTH, 6 * D), 0.02),
        'g_norm1': 1.0 + nrm(ks[6], (DEPTH, D), 0.02),
        'g_norm2': 1.0 + nrm(ks[7], (DEPTH, D), 0.02),
        'w_in': nrm(ks[8], (DEPTH, D, IN_WIDTH), D ** -0.5),
        'w_out': nrm(ks[9], (DEPTH, MIX_WIDTH, D), MIX_WIDTH ** -0.5),
        'w_fnet': nrm(ks[10], (DEPTH, N_FOURIER_GROUPS, FOURIER_GROUP_DIM, FOURIER_GROUP_DIM), FOURIER_GROUP_DIM ** -0.5),
        'lambda_q1': nrm(ks[11], (DEPTH, DIFF_QK_DIM), 0.1),
        'lambda_k1': nrm(ks[12], (DEPTH, DIFF_QK_DIM), 0.1),
        'lambda_q2': nrm(ks[13], (DEPTH, DIFF_QK_DIM), 0.1),
        'lambda_k2': nrm(ks[14], (DEPTH, DIFF_QK_DIM), 0.1),
        'g_subln': 1.0 + nrm(ks[15], (DEPTH, DIFF_V_DIM), 0.02),
        'g_qnorm': 1.0 + nrm(ks[16], (DEPTH, HEAD_DIM), 0.02),
        'g_knorm': 1.0 + nrm(ks[17], (DEPTH, HEAD_DIM), 0.02),
        'w_router': nrm(ks[18], (DEPTH, D, N_EXPERTS), D ** -0.5),
        'b_router': nrm(ks[19], (DEPTH, N_EXPERTS), 0.01),
        'w_gate_up': nrm(ks[20], (DEPTH, N_EXPERTS, D, 2 * EXPERT_FF), D ** -0.5),
        'b_gate_up': nrm(ks[21], (DEPTH, N_EXPERTS, 2 * EXPERT_FF), 0.02),
        'w_down': nrm(ks[22], (DEPTH, N_EXPERTS, EXPERT_FF, D), EXPERT_FF ** -0.5),
        'b_down': nrm(ks[23], (DEPTH, N_EXPERTS, D), 0.02),
        'g_final': 1.0 + nrm(ks[24], (D,), 0.02),
    }


def reference(x, c, ctx, c_ctx, w_mod, b_mod, g_norm1, g_norm2, w_in, w_out, w_fnet, lambda_q1, lambda_k1, lambda_q2, lambda_k2, g_subln, g_qnorm, g_knorm, w_router, b_router, w_gate_up, b_gate_up, w_down, b_down, g_final):
    B, L, D = x.shape
    rows = L // GRID_W
    row_ids = jnp.repeat(jnp.arange(rows, dtype=jnp.float32), GRID_W)
    col_ids = jnp.tile(jnp.arange(GRID_W, dtype=jnp.float32), rows)
    rope_d = axial_rope_tables(row_ids, col_ids, DIFF_QK_DIM)
    rope_g = axial_rope_tables(row_ids, col_ids, HEAD_DIM)
    xc = ctx
    for l in range(DEPTH):
        need_ctx = l < DEPTH - 1
        mod = jax.nn.silu(c) @ w_mod[l] + b_mod[l]
        modc = jax.nn.silu(c_ctx) @ w_mod[l] + b_mod[l]
        sh1, sc1, gt1, sh2, sc2, gt2 = jnp.split(mod[:, None, :], 6, axis=-1)
        sh1c, sc1c, gt1c, sh2c, sc2c, gt2c = jnp.split(modc, 6, axis=-1)
        lam_init = 0.8 - 0.6 * math.exp(-0.3 * l)
        lam = (jnp.exp(jnp.sum(lambda_q1[l].astype(jnp.float32) * lambda_k1[l].astype(jnp.float32)))
               - jnp.exp(jnp.sum(lambda_q2[l].astype(jnp.float32) * lambda_k2[l].astype(jnp.float32))) + lam_init)
        h = modulate(rms_norm(x, g_norm1[l]), sh1, sc1)
        hc = modulate(rms_norm(xc, g_norm1[l]), sh1c, sc1c)
        o, oc = token_mixers(h, hc, need_ctx, w_in[l], w_out[l], w_fnet[l], lam, lam_init, g_subln[l], g_qnorm[l], g_knorm[l], rope_d, rope_g)
        x = x + gt1 * o
        h2 = modulate(rms_norm(x, g_norm2[l]), sh2, sc2)
        if need_ctx:
            xc = xc + gt1c * oc
            h2c = modulate(rms_norm(xc, g_norm2[l]), sh2c, sc2c)
            n_ctx = B * xc.shape[1]
            tokens = jnp.concatenate([h2c.reshape(-1, D), h2.reshape(-1, D)], axis=0)
            y = moe_ffn(tokens, w_router[l], b_router[l], w_gate_up[l], b_gate_up[l], w_down[l], b_down[l])
            xc = xc + gt2c * y[:n_ctx].reshape(xc.shape)
            x = x + gt2 * y[n_ctx:].reshape(x.shape)
        else:
            y = moe_ffn(h2.reshape(-1, D), w_router[l], b_router[l], w_gate_up[l], b_gate_up[l], w_down[l], b_down[l])
            x = x + gt2 * y.reshape(x.shape)
    return rms_norm(x, g_final)
```

```python
import functools
import math

import numpy as np
import jax
import jax.numpy as jnp
from jax import lax
from jax.experimental import pallas as pl
from jax.experimental.pallas import tpu as pltpu

F32 = jnp.float32
BF16 = jnp.bfloat16

GRID_W = 64
HEAD_DIM = 64
N_FOURIER_GROUPS = 4
FOURIER_GROUP_DIM = 64
FOURIER_WIDTH = N_FOURIER_GROUPS * FOURIER_GROUP_DIM
N_DIFF_HEADS = 6
DIFF_QK_DIM = HEAD_DIM // 2
DIFF_WIDTH = N_DIFF_HEADS * HEAD_DIM
N_GQA_Q_HEADS = 6
N_GQA_KV_HEADS = 2
GQA_GROUP = N_GQA_Q_HEADS // N_GQA_KV_HEADS
GQA_WIDTH = N_GQA_Q_HEADS * HEAD_DIM
GQA_KV_WIDTH = N_GQA_KV_HEADS * HEAD_DIM
N_EXPERTS = 32
TOP_K = 4
EXPERT_FF = 1024
SWIGLU_ALPHA = 1.702
SWIGLU_LIMIT = 7.0
ROPE_THETA = 10000.0
NORM_EPS = 1e-6
SUBLN_EPS = 1e-5
DIFF_SCALE = DIFF_QK_DIM ** -0.5
GQA_SCALE = HEAD_DIM ** -0.5

LANES = 128
SUBLANES = 8
VMEM_LIMIT = 48 * 1024 * 1024

ROUTER_PAD = LANES
NEG_BIG = -1e30
MOE_BLOCK = 256
FFT_L2 = 128


def _pick(n, prefs, *offsets):
    g = n
    for o in offsets:
        g = math.gcd(g, o)
    for p in prefs:
        if g % p == 0:
            return p
    return g


def _cparams(sem):
    return pltpu.CompilerParams(dimension_semantics=sem, vmem_limit_bytes=VMEM_LIMIT)


def _mod_kernel(c_ref, w_ref, b_ref, o_ref):
    c = c_ref[...]
    s = c * jax.nn.sigmoid(c)
    o_ref[0] = jnp.dot(s, w_ref[0], precision=lax.Precision.HIGHEST,
                       preferred_element_type=F32) + b_ref[0]


def _mod_call(cc, w_mod, b_mod):
    depth, d, n6 = w_mod.shape
    tn = _pick(n6, (1536, 1024, 512, 256, 128))
    return pl.pallas_call(
        _mod_kernel,
        grid=(depth, n6 // tn),
        in_specs=[
            pl.BlockSpec((SUBLANES, d), lambda l, j: (0, 0)),
            pl.BlockSpec((1, d, tn), lambda l, j: (l, 0, j)),
            pl.BlockSpec((1, 1, tn), lambda l, j: (l, 0, j)),
        ],
        out_specs=pl.BlockSpec((1, SUBLANES, tn), lambda l, j: (l, 0, j)),
        out_shape=jax.ShapeDtypeStruct((depth, SUBLANES, n6), F32),
        compiler_params=_cparams(("arbitrary", "arbitrary")),
    )(cc, w_mod, b_mod.reshape(depth, 1, n6))


_C_F = 0
_C_DQ = _C_F + FOURIER_WIDTH
_C_DK = _C_DQ + DIFF_WIDTH
_C_DV = _C_DK + DIFF_WIDTH
_C_GQ = _C_DV + DIFF_WIDTH
_C_GK = _C_GQ + GQA_WIDTH
_C_GV = _C_GK + GQA_KV_WIDTH
_C_DQS = _C_GV + GQA_KV_WIDTH
_C_DKS = _C_DQS + DIFF_WIDTH
_C_GQS = _C_DKS + DIFF_WIDTH
_C_GKS = _C_GQS + GQA_WIDTH
_C_END = _C_GKS + GQA_KV_WIDTH


def _swap_halves_cols(w, chunk):
    k, n = w.shape
    w = w.reshape(k, n // chunk, 2, chunk // 2)
    return w[:, :, ::-1, :].reshape(k, n)


def _head_mean_sq(z, bd):
    q2 = z * z
    hi = q2.astype(BF16)
    lo = (q2 - hi.astype(F32)).astype(BF16)
    return (jnp.dot(hi, bd, preferred_element_type=F32)
            + jnp.dot(lo, bd, preferred_element_type=F32))


def _in_kernel(x_ref, g_ref, m_ref, w_ref, cd_ref, sd_ref, cg_ref, sg_ref,
               gq_ref, gk_ref, bdq_ref, bdk_ref,
               f_ref, dq_ref, dk_ref, dv_ref, gqo_ref, gko_ref, gvo_ref):
    x = x_ref[...]
    ms = jnp.mean(x * x, axis=-1, keepdims=True)
    y = x * lax.rsqrt(ms + NORM_EPS) * g_ref[...]
    h = y * (1.0 + m_ref[0, 1:2, :]) + m_ref[0, 0:1, :]
    z = jnp.dot(h.astype(BF16), w_ref[...], preferred_element_type=F32)

    f_ref[...] = z[:, _C_F:_C_DQ].astype(BF16)
    dv_ref[...] = z[:, _C_DV:_C_GQ].astype(BF16)
    gvo_ref[...] = z[:, _C_GV:_C_DQS].astype(BF16)

    cd = cd_ref[...]
    sd = sd_ref[...]
    cd3 = jnp.concatenate([cd, cd, cd], axis=1)
    sd3 = jnp.concatenate([sd, sd, sd], axis=1)
    dq = z[:, _C_DQ:_C_DK] * cd3 + z[:, _C_DQS:_C_DKS] * sd3
    dq_ref[...] = (dq * DIFF_SCALE).astype(BF16)
    dk = z[:, _C_DK:_C_DV] * cd3 + z[:, _C_DKS:_C_GQS] * sd3
    dk_ref[...] = dk.astype(BF16)

    cg = cg_ref[...]
    sg = sg_ref[...]
    cg3 = jnp.concatenate([cg, cg, cg], axis=1)
    sg3 = jnp.concatenate([sg, sg, sg], axis=1)
    zq = z[:, _C_GQ:_C_GK]
    rq = lax.rsqrt(_head_mean_sq(zq, bdq_ref[...]) + NORM_EPS)
    gq = rq * (zq * (gq_ref[0:1, :] * cg3) + z[:, _C_GQS:_C_GKS] * (gq_ref[1:2, :] * sg3))
    gqo_ref[...] = (gq * GQA_SCALE).astype(BF16)
    zk = z[:, _C_GK:_C_GV]
    rk = lax.rsqrt(_head_mean_sq(zk, bdk_ref[...]) + NORM_EPS)
    gk = rk * (zk * (gk_ref[0:1, :] * cg) + z[:, _C_GKS:_C_END] * (gk_ref[1:2, :] * sg))
    gko_ref[...] = gk.astype(BF16)


def _in_call(xall, g1, mods, w_ext, tabs, gq2, gk2, bdq, bdk, n_ctx_rows):
    r, d = xall.shape
    tm = _pick(r, (256, 128, 64, 32, 16), n_ctx_rows)
    nct = n_ctx_rows // tm
    cd, sd, cg, sg = tabs
    row = lambda i: (i, 0)
    const = lambda i: (0, 0)
    widths = (FOURIER_WIDTH, DIFF_WIDTH, DIFF_WIDTH, DIFF_WIDTH, GQA_WIDTH, GQA_KV_WIDTH, GQA_KV_WIDTH)
    return pl.pallas_call(
        _in_kernel,
        grid=(r // tm,),
        in_specs=[
            pl.BlockSpec((tm, d), row),
            pl.BlockSpec((1, d), const),
            pl.BlockSpec((1, 6, d), lambda i: (jnp.where(i < nct, 1, 0), 0, 0)),
            pl.BlockSpec((d, _C_END), const),
            pl.BlockSpec((tm, LANES), row),
            pl.BlockSpec((tm, LANES), row),
            pl.BlockSpec((tm, LANES), row),
            pl.BlockSpec((tm, LANES), row),
            pl.BlockSpec((2, GQA_WIDTH), const),
            pl.BlockSpec((2, GQA_KV_WIDTH), const),
            pl.BlockSpec((GQA_WIDTH, GQA_WIDTH), const),
            pl.BlockSpec((GQA_KV_WIDTH, GQA_KV_WIDTH), const),
        ],
        out_specs=[pl.BlockSpec((tm, w), row) for w in widths],
        out_shape=[jax.ShapeDtypeStruct((r, w), BF16) for w in widths],
        compiler_params=_cparams(("arbitrary",)),
    )(xall, g1.reshape(1, d), mods, w_ext, cd, sd, cg, sg, gq2, gk2, bdq, bdk)


def _rope_tables(n_ctx, seq):
    rows = seq // GRID_W
    row_ids = jnp.repeat(jnp.arange(rows, dtype=F32), GRID_W)
    col_ids = jnp.tile(jnp.arange(GRID_W, dtype=F32), rows)

    def tab(rot_dim):
        n = rot_dim // 4
        inv = ROPE_THETA ** (-jnp.arange(n, dtype=F32) / n)
        ang = jnp.concatenate([row_ids[:, None] * inv, col_ids[:, None] * inv], axis=-1)
        cos, sin = jnp.cos(ang), jnp.sin(ang)
        reps = LANES // rot_dim
        c = jnp.tile(jnp.concatenate([cos, cos], axis=-1), (1, reps))
        s = jnp.tile(jnp.concatenate([-sin, sin], axis=-1), (1, reps))
        c = jnp.concatenate([jnp.ones((n_ctx, LANES), F32), c], axis=0)
        s = jnp.concatenate([jnp.zeros((n_ctx, LANES), F32), s], axis=0)
        return c, s

    cd, sd = tab(DIFF_QK_DIM)
    cg, sg = tab(HEAD_DIM)
    return cd, sd, cg, sg


def _block_diag_mean(width, chunk):
    a = np.kron(np.eye(width // chunk), np.full((chunk, chunk), 1.0 / chunk))
    return jnp.asarray(a, dtype=BF16)


def _attn_kernel(*refs, diff, tq, tk, nk, lam_scale):
    if diff:
        q_ref, k_ref, v_ref, lamp_ref, gsub_ref, o_ref, qs_sc, m_sc, acc_sc = refs
    else:
        q_ref, k_ref, v_ref, o_ref, qs_sc, m_sc, acc_sc = refs

    if diff:
        q = q_ref[0]
        lane = lax.broadcasted_iota(jnp.int32, q.shape, 1)
        zero = jnp.zeros_like(q)
        qs_sc[0:tq, :] = jnp.where(lane < DIFF_QK_DIM, q, zero)
        qs_sc[tq:2 * tq, :] = jnp.where(lane >= DIFF_QK_DIM, q, zero)
        groups = 2
    else:
        groups = GQA_GROUP
        for g in range(groups):
            qs_sc[g * tq:(g + 1) * tq, :] = q_ref[0, g]
    m_sc[...] = jnp.full(m_sc.shape, -jnp.inf, F32)
    acc_sc[...] = jnp.zeros(acc_sc.shape, F32)

    def body(j, carry):
        start = pl.multiple_of(j * tk, tk)
        kc = k_ref[0, pl.ds(start, tk), :]
        vc = v_ref[0, pl.ds(start, tk), :]
        s = lax.dot_general(qs_sc[...], kc, (((1,), (1,)), ((), ())),
                            preferred_element_type=F32)
        m_old = m_sc[...]
        m_new = jnp.maximum(m_old, jnp.max(s, axis=1, keepdims=True))
        alpha = jnp.exp(m_old - m_new)
        p = jnp.exp(s - m_new)
        acc_sc[...] = acc_sc[...] * alpha + jnp.dot(p.astype(BF16), vc,
                                                    preferred_element_type=F32)
        m_sc[...] = m_new
        return carry

    lax.fori_loop(0, nk, body, 0)

    acc = acc_sc[...]
    o = acc[:, :HEAD_DIM] / acc[:, HEAD_DIM:HEAD_DIM + 1]
    if diff:
        lp = lamp_ref[...]
        lam = (jnp.exp(jnp.sum(lp[0:1] * lp[1:2], axis=1, keepdims=True))
               - jnp.exp(jnp.sum(lp[2:3] * lp[3:4], axis=1, keepdims=True)) + (1.0 - lam_scale))
        od = o[0:tq] - lam * o[tq:2 * tq]
        ms = jnp.mean(od * od, axis=-1, keepdims=True)
        o_ref[0] = (od * lax.rsqrt(ms + SUBLN_EPS) * gsub_ref[...] * lam_scale).astype(o_ref.dtype)
    else:
        for g in range(groups):
            o_ref[0, g] = o[g * tq:(g + 1) * tq].astype(o_ref.dtype)


def _attn_call(q, k, v, *, diff, q_row0, n_q, n_k, lamp=None, gsub=None, lam_scale=1.0):
    nh = q.shape[0]
    tq = _pick(n_q, (256, 128, 64, 32, 16), q_row0)
    qb0 = q_row0 // tq
    tk = _pick(n_k, (1280, 1024, 512, 256, 128, 64, 32, 16))
    nk = n_k // tk
    groups = 2 if diff else GQA_GROUP
    kernel = functools.partial(_attn_kernel, diff=diff, tq=tq, tk=tk, nk=nk, lam_scale=lam_scale)
    kv_specs = [
        pl.BlockSpec((1, n_k, HEAD_DIM), lambda h, i: (h, 0, 0)),
        pl.BlockSpec((1, n_k, LANES), lambda h, i: (h, 0, 0)),
    ]
    if diff:
        in_specs = [pl.BlockSpec((1, tq, HEAD_DIM), lambda h, i: (h, i + qb0, 0))] + kv_specs + [
            pl.BlockSpec((4, DIFF_QK_DIM), lambda h, i: (0, 0)),
            pl.BlockSpec((1, HEAD_DIM), lambda h, i: (0, 0)),
        ]
        args = (q, k, v, lamp, gsub)
        out_spec = pl.BlockSpec((1, tq, HEAD_DIM), lambda h, i: (h, i, 0))
        out_shape = jax.ShapeDtypeStruct((nh, n_q, HEAD_DIM), BF16)
    else:
        in_specs = [pl.BlockSpec((1, groups, tq, HEAD_DIM), lambda h, i: (h, 0, i + qb0, 0))] + kv_specs
        args = (q, k, v)
        out_spec = pl.BlockSpec((1, groups, tq, HEAD_DIM), lambda h, i: (h, 0, i, 0))
        out_shape = jax.ShapeDtypeStruct((nh, groups, n_q, HEAD_DIM), BF16)
    return pl.pallas_call(
        kernel,
        grid=(nh, n_q // tq),
        in_specs=in_specs,
        out_specs=out_spec,
        out_shape=out_shape,
        scratch_shapes=[
            pltpu.VMEM((groups * tq, HEAD_DIM), BF16),
            pltpu.VMEM((groups * tq, 1), F32),
            pltpu.VMEM((groups * tq, LANES), F32),
        ],
        compiler_params=_cparams(("arbitrary", "arbitrary")),
    )(*args)


def _dft_stage1_kernel(f1_ref, u_ref, z_ref):
    z = jnp.dot(f1_ref[...], u_ref[...], preferred_element_type=F32)
    l1 = u_ref.shape[0]
    z_ref[0] = z[:l1].astype(z_ref.dtype)
    z_ref[1] = z[l1:].astype(z_ref.dtype)


def _dft_stage1_call(f1s, u2d):
    l1, n = u2d.shape
    tn = _pick(n, (2048, 1024, 512, 256, 128))
    return pl.pallas_call(
        _dft_stage1_kernel,
        grid=(n // tn,),
        in_specs=[pl.BlockSpec((2 * l1, l1), lambda j: (0, 0)),
                  pl.BlockSpec((l1, tn), lambda j: (0, j))],
        out_specs=pl.BlockSpec((2, l1, tn), lambda j: (0, 0, j)),
        out_shape=jax.ShapeDtypeStruct((2, l1, n), BF16),
        compiler_params=_cparams(("arbitrary",)),
    )(f1s, u2d)


def _dft_stage2_kernel(e_ref, z_ref, cs_ref, w_ref, o_ref, *, kb, scale, real_only):
    for b in range(kb):
        zr = z_ref[0, b]
        if real_only:
            e = e_ref[b]
            x = jnp.dot(e, zr, preferred_element_type=F32)
            l2 = zr.shape[0]
            xcat = jnp.concatenate([x[:l2], x[l2:]], axis=1)
        else:
            zi = z_ref[1, b]
            rhs = jnp.concatenate(
                [jnp.concatenate([zr, zi], axis=1), jnp.concatenate([zi, -zr], axis=1)], axis=0)
            xcat = jnp.dot(e_ref[b], rhs, preferred_element_type=F32)
        y1 = jnp.dot(xcat.astype(BF16), cs_ref[...], preferred_element_type=F32)
        y = jnp.dot(y1.astype(BF16), w_ref[...], preferred_element_type=F32)
        o_ref[b] = (y * scale).astype(o_ref.dtype)


def _dft_stage2_call(e2, z, cs, wbd, *, scale, real_only):
    nz, l1, l2, cw = z.shape
    kb = _pick(l1, (8, 4, 2, 1))
    kernel = functools.partial(_dft_stage2_kernel, kb=kb, scale=scale, real_only=real_only)
    return pl.pallas_call(
        kernel,
        grid=(l1 // kb,),
        in_specs=[pl.BlockSpec((kb,) + e2.shape[1:], lambda i: (i, 0, 0)),
                  pl.BlockSpec((nz, kb, l2, cw), lambda i: (0, i, 0, 0)),
                  pl.BlockSpec(cs.shape, lambda i: (0, 0)),
                  pl.BlockSpec(wbd.shape, lambda i: (0, 0))],
        out_specs=pl.BlockSpec((kb, l2, cw), lambda i: (i, 0, 0)),
        out_shape=jax.ShapeDtypeStruct((l1, l2, cw), BF16),
        compiler_params=_cparams(("arbitrary",)),
    )(e2, z, cs, wbd)


@functools.lru_cache(maxsize=None)
def _dft_tables(n):
    l2 = FFT_L2 if n > 2 * FFT_L2 and n % FFT_L2 == 0 else n
    l1 = n // l2
    k1 = np.arange(l1)
    ang1 = 2.0 * np.pi * ((k1[:, None] * k1[None, :]) % l1) / l1
    f1s = np.concatenate([np.cos(ang1), -np.sin(ang1)], axis=0)
    k = k1[:, None] + l1 * np.arange(l2)[None, :]
    n2 = np.arange(l2)
    ang = 2.0 * np.pi * ((k[:, :, None] * n2[None, None, :]) % n) / n
    if l1 == 1:
        e2 = np.concatenate([np.cos(ang), -np.sin(ang)], axis=1)
    else:
        e2 = np.concatenate([np.cos(ang), np.sin(ang)], axis=2)
    c = np.arange(FOURIER_GROUP_DIM)
    angc = 2.0 * np.pi * ((c[:, None] * c[None, :]) % FOURIER_GROUP_DIM) / FOURIER_GROUP_DIM
    eye = np.eye(N_FOURIER_GROUPS)
    cs = np.concatenate([np.kron(eye, np.cos(angc)), np.kron(eye, np.sin(angc))], axis=0)
    return l1, l2, f1s.astype(np.float32), e2.astype(np.float32), cs.astype(np.float32)


def _fnet_call(u, wbd):
    n, cw = u.shape
    l1, l2, f1s, e2, cs = _dft_tables(n)
    scale = 1.0 / math.sqrt(n * FOURIER_GROUP_DIM)
    e2 = jnp.asarray(e2, dtype=BF16)
    cs = jnp.asarray(cs, dtype=BF16)
    if l1 == 1:
        z = u.reshape(1, 1, l2, cw)
        y = _dft_stage2_call(e2, z, cs, wbd, scale=scale, real_only=True)
        return y.reshape(n, cw)
    z = _dft_stage1_call(jnp.asarray(f1s, dtype=BF16), u.reshape(l1, l2 * cw))
    y = _dft_stage2_call(e2, z.reshape(2, l1, l2, cw), cs, wbd, scale=scale, real_only=False)
    return jnp.transpose(y, (1, 0, 2)).reshape(n, cw)


def _out_kernel(mix_ref, x_ref, w_ref, m_ref, g2_ref, wr_ref, br_ref,
                xo_ref, h2_ref, gmat_ref, g4_ref, i4_ref):
    o = jnp.dot(mix_ref[...], w_ref[...], preferred_element_type=F32)
    xn = x_ref[...] + m_ref[0, 2:3, :] * o
    xo_ref[...] = xn
    ms = jnp.mean(xn * xn, axis=-1, keepdims=True)
    h2 = xn * lax.rsqrt(ms + NORM_EPS) * g2_ref[...]
    h2 = h2 * (1.0 + m_ref[0, 4:5, :]) + m_ref[0, 3:4, :]
    h2_ref[...] = h2
    logits = jnp.dot(h2, wr_ref[...], precision=lax.Precision.HIGHEST,
                     preferred_element_type=F32) + br_ref[...]
    iota = lax.broadcasted_iota(jnp.int32, logits.shape, 1).astype(F32)
    vals, idxs = [], []
    l = logits
    for _ in range(TOP_K):
        mx = jnp.max(l, axis=1, keepdims=True)
        ik = jnp.min(jnp.where(l == mx, iota, float(ROUTER_PAD)), axis=1, keepdims=True)
        vals.append(mx)
        idxs.append(ik)
        l = jnp.where(iota == ik, -jnp.inf, l)
    es = [jnp.exp(v - vals[0]) for v in vals]
    den = es[0] + es[1] + es[2] + es[3]
    gs = [e / den for e in es]
    gmat = jnp.zeros(logits.shape, F32)
    for ik, g in zip(idxs, gs):
        gmat = jnp.where(iota == ik, g, gmat)
    gmat_ref[...] = gmat
    lane4 = lax.broadcasted_iota(jnp.int32, g4_ref.shape, 1)
    g4 = jnp.zeros(g4_ref.shape, F32)
    i4 = jnp.zeros(g4_ref.shape, F32)
    for k in range(TOP_K):
        g4 = jnp.where(lane4 == k, gs[k], g4)
        i4 = jnp.where(lane4 == k, idxs[k], i4)
    g4_ref[...] = g4
    i4_ref[...] = i4.astype(jnp.int32)


def _out_call(mix, xsrc, x_row0, w_out, mods, g2, wr, br, n_ctx_rows):
    n, d = mix.shape
    tm = _pick(n, (256, 128, 64, 32, 16), x_row0, n_ctx_rows)
    xb0 = x_row0 // tm
    nct = n_ctx_rows // tm
    row = lambda i: (i, 0)
    const = lambda i: (0, 0)
    return pl.pallas_call(
        _out_kernel,
        grid=(n // tm,),
        in_specs=[
            pl.BlockSpec((tm, d), row),
            pl.BlockSpec((tm, d), lambda i: (i + xb0, 0)),
            pl.BlockSpec((d, d), const),
            pl.BlockSpec((1, 6, d), lambda i: (jnp.where(i < nct, 1, 0), 0, 0)),
            pl.BlockSpec((1, d), const),
            pl.BlockSpec((d, ROUTER_PAD), const),
            pl.BlockSpec((1, ROUTER_PAD), const),
        ],
        out_specs=[
            pl.BlockSpec((tm, d), row),
            pl.BlockSpec((tm, d), row),
            pl.BlockSpec((tm, ROUTER_PAD), row),
            pl.BlockSpec((tm, TOP_K), row),
            pl.BlockSpec((tm, TOP_K), row),
        ],
        out_shape=[
            jax.ShapeDtypeStruct((n, d), F32),
            jax.ShapeDtypeStruct((n, d), F32),
            jax.ShapeDtypeStruct((n, ROUTER_PAD), F32),
            jax.ShapeDtypeStruct((n, TOP_K), F32),
            jax.ShapeDtypeStruct((n, TOP_K), jnp.int32),
        ],
        compiler_params=_cparams(("arbitrary",)),
    )(mix, xsrc, w_out, mods, g2.reshape(1, d), wr, br)


def _rank_kernel(g_ref, rank_ref, cnt_ref, carry_sc):
    @pl.when(pl.program_id(0) == 0)
    def _():
        carry_sc[...] = jnp.zeros(carry_sc.shape, F32)

    self32 = jnp.where(g_ref[...] > 0.0, 1.0, 0.0)
    sel = self32.astype(BF16)
    tm = sel.shape[0]
    r = lax.broadcasted_iota(jnp.int32, (tm, tm), 0)
    c = lax.broadcasted_iota(jnp.int32, (tm, tm), 1)
    tri = jnp.where(c < r, 1.0, 0.0).astype(BF16)
    excl = jnp.dot(tri, sel, preferred_element_type=F32) + carry_sc[0:1, :]
    rank_ref[...] = excl.astype(jnp.int32)
    tot = carry_sc[0:1, :] + jnp.sum(self32, axis=0, keepdims=True)
    carry_sc[...] = jnp.broadcast_to(tot, carry_sc.shape)
    cnt_ref[...] = jnp.broadcast_to(tot, cnt_ref.shape).astype(jnp.int32)


def _rank_call(gmat):
    n, w = gmat.shape
    tm = _pick(n, (256, 128, 64, 32, 16))
    return pl.pallas_call(
        _rank_kernel,
        grid=(n // tm,),
        in_specs=[pl.BlockSpec((tm, w), lambda i: (i, 0))],
        out_specs=[pl.BlockSpec((tm, w), lambda i: (i, 0)),
                   pl.BlockSpec((SUBLANES, w), lambda i: (0, 0))],
        out_shape=[jax.ShapeDtypeStruct((n, w), jnp.int32),
                   jax.ShapeDtypeStruct((SUBLANES, w), jnp.int32)],
        scratch_shapes=[pltpu.VMEM((SUBLANES, w), F32)],
        compiler_params=_cparams(("arbitrary",)),
    )(gmat)


def _dispatch_kernel(dest_ref, h_ref, xs_in_ref, xs_ref, sem, *, tt):
    del xs_in_ref
    base = pl.program_id(0) * tt

    def copy(p):
        t = base + p // TOP_K
        return pltpu.make_async_copy(h_ref.at[pl.ds(t, 1)], xs_ref.at[pl.ds(dest_ref[base * TOP_K + p], 1)], sem)

    def issue(p, c):
        copy(p).start()
        return c

    def drain(p, c):
        copy(p).wait()
        return c

    lax.fori_loop(0, tt * TOP_K, issue, 0)
    lax.fori_loop(0, tt * TOP_K, drain, 0)


def _dispatch_call(dest_flat, h2, n_slots):
    n, d = h2.shape
    tt = _pick(n, (512, 256, 128, 64, 32, 16))
    xs0 = jnp.zeros((n_slots, d), F32)
    return pl.pallas_call(
        functools.partial(_dispatch_kernel, tt=tt),
        grid_spec=pltpu.PrefetchScalarGridSpec(
            num_scalar_prefetch=1,
            grid=(n // tt,),
            in_specs=[pl.BlockSpec(memory_space=pl.ANY), pl.BlockSpec(memory_space=pl.ANY)],
            out_specs=pl.BlockSpec(memory_space=pl.ANY),
            scratch_shapes=[pltpu.SemaphoreType.DMA(())],
        ),
        out_shape=jax.ShapeDtypeStruct((n_slots, d), F32),
        input_output_aliases={2: 0},
        compiler_params=_cparams(("arbitrary",)),
    )(dest_flat, h2, xs0)


def _expert_kernel(be_ref, nb_ref, x_ref, wgu_ref, bgu_ref, wd_ref, bd_ref, y_ref):
    @pl.when(pl.program_id(0) < nb_ref[0])
    def _():
        x = x_ref[...].astype(BF16)
        gu = jnp.dot(x, wgu_ref[0], preferred_element_type=F32) + bgu_ref[0]
        glu = jnp.minimum(gu[:, :EXPERT_FF], SWIGLU_LIMIT)
        lin = jnp.clip(gu[:, EXPERT_FF:], -SWIGLU_LIMIT, SWIGLU_LIMIT)
        act = glu * jax.nn.sigmoid(SWIGLU_ALPHA * glu) * (lin + 1.0)
        y_ref[...] = jnp.dot(act.astype(BF16), wd_ref[0], preferred_element_type=F32) + bd_ref[0]

    @pl.when(pl.program_id(0) >= nb_ref[0])
    def _():
        y_ref[...] = jnp.zeros(y_ref.shape, F32)


def _expert_call(blk_e, nb_used, xs, wgu, bgu, wd, bd):
    p, d = xs.shape
    nb = p // MOE_BLOCK
    ne, _, ff2 = wgu.shape
    blk = lambda b, be, nbu: (jnp.minimum(b, nbu[0] - 1), 0)
    wsel = lambda b, be, nbu: (be[jnp.minimum(b, nbu[0] - 1)], 0, 0)
    return pl.pallas_call(
        _expert_kernel,
        grid_spec=pltpu.PrefetchScalarGridSpec(
            num_scalar_prefetch=2,
            grid=(nb,),
            in_specs=[
                pl.BlockSpec((MOE_BLOCK, d), blk),
                pl.BlockSpec((1, d, ff2), wsel),
                pl.BlockSpec((1, 1, ff2), wsel),
                pl.BlockSpec((1, ff2 // 2, d), wsel),
                pl.BlockSpec((1, 1, d), wsel),
            ],
            out_specs=pl.BlockSpec((MOE_BLOCK, d), lambda b, be, nbu: (b, 0)),
        ),
        out_shape=jax.ShapeDtypeStruct((p, d), F32),
        compiler_params=_cparams(("arbitrary",)),
    )(blk_e, nb_used, xs, wgu, bgu.reshape(ne, 1, ff2), wd, bd.reshape(ne, 1, d))


def _combine_kernel(dest_ref, ys_ref, x_ref, g4_ref, m_ref, gf_ref, o_ref, buf, sem, *, tm, final):
    base = pl.program_id(0) * tm * TOP_K

    def copy(p):
        t = p // TOP_K
        k = p % TOP_K
        return pltpu.make_async_copy(ys_ref.at[pl.ds(dest_ref[base + p], 1)],
                                     buf.at[k, pl.ds(t, 1)], sem)

    def issue(p, c):
        copy(p).start()
        return c

    def drain(p, c):
        copy(p).wait()
        return c

    lax.fori_loop(0, tm * TOP_K, issue, 0)
    lax.fori_loop(0, tm * TOP_K, drain, 0)

    g4 = g4_ref[...]
    y = buf[0] * g4[:, 0:1]
    for k in range(1, TOP_K):
        y = y + buf[k] * g4[:, k:k + 1]
    xo = x_ref[...] + m_ref[0, 5:6, :] * y
    if final:
        ms = jnp.mean(xo * xo, axis=-1, keepdims=True)
        xo = xo * lax.rsqrt(ms + NORM_EPS) * gf_ref[...]
    o_ref[...] = xo


def _combine_call(dest_flat, ys, x, g4, mods, g_final, n_ctx_rows, final):
    n, d = x.shape
    tm = _pick(n, (256, 128, 64, 32, 16), n_ctx_rows)
    nct = n_ctx_rows // tm
    row = lambda i, dref: (i, 0)
    return pl.pallas_call(
        functools.partial(_combine_kernel, tm=tm, final=final),
        grid_spec=pltpu.PrefetchScalarGridSpec(
            num_scalar_prefetch=1,
            grid=(n // tm,),
            in_specs=[
                pl.BlockSpec(memory_space=pl.ANY),
                pl.BlockSpec((tm, d), row),
                pl.BlockSpec((tm, TOP_K), row),
                pl.BlockSpec((1, 6, d), lambda i, dref: (jnp.where(i < nct, 1, 0), 0, 0)),
                pl.BlockSpec((1, d), lambda i, dref: (0, 0)),
            ],
            out_specs=pl.BlockSpec((tm, d), row),
            scratch_shapes=[pltpu.VMEM((TOP_K, tm, d), F32), pltpu.SemaphoreType.DMA(())],
        ),
        out_shape=jax.ShapeDtypeStruct((n, d), F32),
        compiler_params=_cparams(("arbitrary",)),
    )(dest_flat, ys, x, g4, mods, g_final.reshape(1, d))


def _moe(x_mid, h2, gmat, g4, i4, mods, n_ctx_rows, wgu, bgu, wd, bd, g_final, final):
    n, d = h2.shape
    rank, cnt = _rank_call(gmat)
    counts = cnt[0, :N_EXPERTS]
    padded = (counts + MOE_BLOCK - 1) // MOE_BLOCK * MOE_BLOCK
    ends = jnp.cumsum(padded)
    starts = ends - padded
    dest = starts[i4] + jnp.take_along_axis(rank, i4, axis=1)
    dest_flat = dest.reshape(-1).astype(jnp.int32)
    n_slots = (n * TOP_K + MOE_BLOCK - 1) // MOE_BLOCK * MOE_BLOCK + N_EXPERTS * MOE_BLOCK
    nb = n_slots // MOE_BLOCK
    blk_e = jnp.minimum(
        jnp.searchsorted(ends, jnp.arange(nb, dtype=jnp.int32) * MOE_BLOCK, side='right'),
        N_EXPERTS - 1).astype(jnp.int32)
    nb_used = (ends[-1] // MOE_BLOCK).astype(jnp.int32).reshape(1)
    xs = _dispatch_call(dest_flat, h2, n_slots)
    ys = _expert_call(blk_e, nb_used, xs, wgu, bgu, wd, bd)
    return _combine_call(dest_flat, ys, x_mid, g4, mods, g_final, n_ctx_rows, final)


def _heads(a, nh):
    r = a.shape[0]
    return jnp.transpose(a.reshape(r, nh, HEAD_DIM), (1, 0, 2))


def _v_ext(v):
    ones = jnp.ones(v.shape[:-1] + (1,), v.dtype)
    zeros = jnp.zeros(v.shape[:-1] + (LANES - HEAD_DIM - 1,), v.dtype)
    return jnp.concatenate([v, ones, zeros], axis=-1)


def _unheads(o):
    n = o.shape[-2]
    o = o.reshape(-1, n, HEAD_DIM)
    return jnp.transpose(o, (1, 0, 2)).reshape(n, -1)


def kernel(x, c, ctx, c_ctx, w_mod, b_mod, g_norm1, g_norm2, w_in, w_out, w_fnet, lambda_q1, lambda_k1, lambda_q2, lambda_k2, g_subln, g_qnorm, g_knorm, w_router, b_router, w_gate_up, b_gate_up, w_down, b_down, g_final):
    bsz, seq, d = x.shape
    assert bsz == 1
    n_ctx = ctx.shape[1]
    depth = w_mod.shape[0]
    r = n_ctx + seq

    cc = jnp.zeros((SUBLANES, d), F32).at[0].set(c[0]).at[1].set(c_ctx)
    mod_all = _mod_call(cc, w_mod, b_mod)
    tabs = _rope_tables(n_ctx, seq)
    bdq = _block_diag_mean(GQA_WIDTH, HEAD_DIM)
    bdk = _block_diag_mean(GQA_KV_WIDTH, HEAD_DIM)

    xall = jnp.concatenate([ctx[0], x[0]], axis=0)
    out = None
    for l in range(depth):
        need_ctx = l < depth - 1
        lam_init = 0.8 - 0.6 * math.exp(-0.3 * l)
        mods = mod_all[l, :2].reshape(2, 6, d)
        wl = w_in[l]
        w_ext = jnp.concatenate([
            wl,
            _swap_halves_cols(wl[:, _C_DQ:_C_DK], DIFF_QK_DIM),
            _swap_halves_cols(wl[:, _C_DK:_C_DV], DIFF_QK_DIM),
            _swap_halves_cols(wl[:, _C_GQ:_C_GK], HEAD_DIM),
            _swap_halves_cols(wl[:, _C_GK:_C_GV], HEAD_DIM),
        ], axis=1).astype(BF16)
        gq = g_qnorm[l]
        gk = g_knorm[l]
        gq_sw = jnp.concatenate([gq[HEAD_DIM // 2:], gq[:HEAD_DIM // 2]])
        gk_sw = jnp.concatenate([gk[HEAD_DIM // 2:], gk[:HEAD_DIM // 2]])
        gq2 = jnp.stack([jnp.tile(gq, N_GQA_Q_HEADS), jnp.tile(gq_sw, N_GQA_Q_HEADS)])
        gk2 = jnp.stack([jnp.tile(gk, N_GQA_KV_HEADS), jnp.tile(gk_sw, N_GQA_KV_HEADS)])

        f, dq, dk, dv, gqo, gko, gvo = _in_call(xall, g_norm1[l], mods, w_ext, tabs, gq2, gk2,
                                                 bdq, bdk, n_ctx)

        dqh = _heads(dq, N_DIFF_HEADS)
        dkh = _heads(dk, N_DIFF_HEADS)
        dvh = _v_ext(_heads(dv, N_DIFF_HEADS))
        gqh = _heads(gqo, N_GQA_Q_HEADS).reshape(N_GQA_KV_HEADS, GQA_GROUP, r, HEAD_DIM)
        gkh = _heads(gko, N_GQA_KV_HEADS)
        gvh = _v_ext(_heads(gvo, N_GQA_KV_HEADS))
        lamp = jnp.stack([lambda_q1[l], lambda_k1[l], lambda_q2[l], lambda_k2[l]]).astype(F32)
        gsub = g_subln[l].reshape(1, HEAD_DIM)
        wbd = jnp.zeros((FOURIER_WIDTH, FOURIER_WIDTH), F32)
        for g in range(N_FOURIER_GROUPS):
            sl = slice(g * FOURIER_GROUP_DIM, (g + 1) * FOURIER_GROUP_DIM)
            wbd = wbd.at[sl, sl].set(w_fnet[l, g])
        wbd = wbd.astype(BF16)

        def mixers(q_row0, n_q, n_k):
            d_o = _attn_call(dqh, dkh, dvh, diff=True, q_row0=q_row0, n_q=n_q, n_k=n_k,
                             lamp=lamp, gsub=gsub, lam_scale=1.0 - lam_init)
            g_o = _attn_call(gqh, gkh, gvh, diff=False, q_row0=q_row0, n_q=n_q, n_k=n_k)
            f_o = _fnet_call(f[q_row0:q_row0 + n_q], wbd)
            return jnp.concatenate([f_o, _unheads(d_o), _unheads(g_o)], axis=1)

        mix = mixers(n_ctx, seq, r)
        if need_ctx:
            mix = jnp.concatenate([mixers(0, n_ctx, n_ctx), mix], axis=0)
            x_row0, nct = 0, n_ctx
        else:
            x_row0, nct = n_ctx, 0

        wr = jnp.zeros((d, ROUTER_PAD), F32).at[:, :N_EXPERTS].set(w_router[l])
        br = jnp.full((1, ROUTER_PAD), NEG_BIG, F32).at[0, :N_EXPERTS].set(b_router[l])
        x_mid, h2, gmat, g4, i4 = _out_call(mix, xall, x_row0, w_out[l].astype(BF16), mods,
                                            g_norm2[l], wr, br, nct)
        xall = _moe(x_mid, h2, gmat, g4, i4, mods, nct,
                    w_gate_up[l].astype(BF16), b_gate_up[l], w_down[l].astype(BF16), b_down[l],
                    g_final, final=not need_ctx)
    return xall.reshape(1, seq, d)
```

```python
import functools
import math

import numpy as np
import jax
import jax.numpy as jnp
from jax import lax
from jax.experimental import pallas as pl
from jax.experimental.pallas import tpu as pltpu

F32 = jnp.float32
BF16 = jnp.bfloat16

GRID_W = 64
HEAD_DIM = 64
N_FOURIER_GROUPS = 4
FOURIER_GROUP_DIM = 64
FOURIER_WIDTH = N_FOURIER_GROUPS * FOURIER_GROUP_DIM
N_DIFF_HEADS = 6
DIFF_QK_DIM = HEAD_DIM // 2
DIFF_WIDTH = N_DIFF_HEADS * HEAD_DIM
N_GQA_Q_HEADS = 6
N_GQA_KV_HEADS = 2
GQA_GROUP = N_GQA_Q_HEADS // N_GQA_KV_HEADS
GQA_WIDTH = N_GQA_Q_HEADS * HEAD_DIM
GQA_KV_WIDTH = N_GQA_KV_HEADS * HEAD_DIM
N_EXPERTS = 32
TOP_K = 4
EXPERT_FF = 1024
SWIGLU_ALPHA = 1.702
SWIGLU_LIMIT = 7.0
ROPE_THETA = 10000.0
NORM_EPS = 1e-6
SUBLN_EPS = 1e-5
LOG2E = math.log2(math.e)
DIFF_SCALE = DIFF_QK_DIM ** -0.5 * LOG2E
GQA_SCALE = HEAD_DIM ** -0.5 * LOG2E

LANES = 128
SUBLANES = 8
VMEM_LIMIT = 56 * 1024 * 1024

ROUTER_PAD = LANES
NEG_BIG = -1e30
MOE_BLOCK = 256
ATTN_TK = 1280
FFT_L2 = 128


def _pick(n, prefs, *offsets):
    g = n
    for o in offsets:
        g = math.gcd(g, o)
    for p in prefs:
        if g % p == 0:
            return p
    return g


def _cparams(sem):
    return pltpu.CompilerParams(dimension_semantics=sem, vmem_limit_bytes=VMEM_LIMIT)


def _mod_kernel(c_ref, w_ref, b_ref, o_ref):
    c = c_ref[...]
    s = c * jax.nn.sigmoid(c)
    o_ref[0] = jnp.dot(s, w_ref[0], precision=lax.Precision.HIGHEST,
                       preferred_element_type=F32) + b_ref[0]


def _mod_call(cc, w_mod, b_mod):
    depth, d, n6 = w_mod.shape
    tn = _pick(n6, (1536, 1024, 512, 256, 128))
    return pl.pallas_call(
        _mod_kernel,
        grid=(depth, n6 // tn),
        in_specs=[
            pl.BlockSpec((SUBLANES, d), lambda l, j: (0, 0)),
            pl.BlockSpec((1, d, tn), lambda l, j: (l, 0, j)),
            pl.BlockSpec((1, 1, tn), lambda l, j: (l, 0, j)),
        ],
        out_specs=pl.BlockSpec((1, SUBLANES, tn), lambda l, j: (l, 0, j)),
        out_shape=jax.ShapeDtypeStruct((depth, SUBLANES, n6), F32),
        compiler_params=_cparams(("arbitrary", "arbitrary")),
    )(cc, w_mod, b_mod.reshape(depth, 1, n6))


_C_F = 0
_C_DQ = _C_F + FOURIER_WIDTH
_C_DK = _C_DQ + DIFF_WIDTH
_C_DV = _C_DK + DIFF_WIDTH
_C_GQ = _C_DV + DIFF_WIDTH
_C_GK = _C_GQ + GQA_WIDTH
_C_GV = _C_GK + GQA_KV_WIDTH
_C_DQS = _C_GV + GQA_KV_WIDTH
_C_DKS = _C_DQS + DIFF_WIDTH
_C_GQS = _C_DKS + DIFF_WIDTH
_C_GKS = _C_GQS + GQA_WIDTH
_C_END = _C_GKS + GQA_KV_WIDTH


def _swap_halves_cols(w, chunk):
    k, n = w.shape
    w = w.reshape(k, n // chunk, 2, chunk // 2)
    return w[:, :, ::-1, :].reshape(k, n)


def _head_mean_sq(z, bd):
    q2 = z * z
    hi = q2.astype(BF16)
    lo = (q2 - hi.astype(F32)).astype(BF16)
    return (jnp.dot(hi, bd, preferred_element_type=F32)
            + jnp.dot(lo, bd, preferred_element_type=F32))


def _in_kernel(x_ref, g_ref, m_ref, w_ref, cd_ref, sd_ref, cg_ref, sg_ref,
               gq_ref, gk_ref, bdq_ref, bdk_ref,
               f_ref, dq_ref, dk_ref, dv_ref, gqo_ref, gko_ref, gvo_ref):
    x = x_ref[...]
    ms = jnp.mean(x * x, axis=-1, keepdims=True)
    y = x * lax.rsqrt(ms + NORM_EPS) * g_ref[...]
    h = y * (1.0 + m_ref[0, 1:2, :]) + m_ref[0, 0:1, :]
    z = jnp.dot(h.astype(BF16), w_ref[...], preferred_element_type=F32)

    f_ref[...] = z[:, _C_F:_C_DQ].astype(BF16)
    dv_ref[...] = z[:, _C_DV:_C_GQ].astype(BF16)
    gvo_ref[...] = z[:, _C_GV:_C_DQS].astype(BF16)

    cd = cd_ref[...]
    sd = sd_ref[...]
    cd3 = jnp.concatenate([cd, cd, cd], axis=1)
    sd3 = jnp.concatenate([sd, sd, sd], axis=1)
    dq = z[:, _C_DQ:_C_DK] * cd3 + z[:, _C_DQS:_C_DKS] * sd3
    dq_ref[...] = (dq * DIFF_SCALE).astype(BF16)
    dk = z[:, _C_DK:_C_DV] * cd3 + z[:, _C_DKS:_C_GQS] * sd3
    dk_ref[...] = dk.astype(BF16)

    cg = cg_ref[...]
    sg = sg_ref[...]
    cg3 = jnp.concatenate([cg, cg, cg], axis=1)
    sg3 = jnp.concatenate([sg, sg, sg], axis=1)
    zq = z[:, _C_GQ:_C_GK]
    rq = lax.rsqrt(_head_mean_sq(zq, bdq_ref[...]) + NORM_EPS)
    gq = rq * (zq * (gq_ref[0:1, :] * cg3) + z[:, _C_GQS:_C_GKS] * (gq_ref[1:2, :] * sg3))
    gqo_ref[...] = (gq * GQA_SCALE).astype(BF16)
    zk = z[:, _C_GK:_C_GV]
    rk = lax.rsqrt(_head_mean_sq(zk, bdk_ref[...]) + NORM_EPS)
    gk = rk * (zk * (gk_ref[0:1, :] * cg) + z[:, _C_GKS:_C_END] * (gk_ref[1:2, :] * sg))
    gko_ref[...] = gk.astype(BF16)


def _in_call(xall, g1, mods, w_ext, tabs, gq2, gk2, bdq, bdk, n_ctx_rows):
    r, d = xall.shape
    tm = _pick(r, (256, 128, 64, 32, 16), n_ctx_rows)
    nct = n_ctx_rows // tm
    cd, sd, cg, sg = tabs
    row = lambda i: (i, 0)
    const = lambda i: (0, 0)
    widths = (FOURIER_WIDTH, DIFF_WIDTH, DIFF_WIDTH, DIFF_WIDTH, GQA_WIDTH, GQA_KV_WIDTH, GQA_KV_WIDTH)
    return pl.pallas_call(
        _in_kernel,
        grid=(r // tm,),
        in_specs=[
            pl.BlockSpec((tm, d), row),
            pl.BlockSpec((1, d), const),
            pl.BlockSpec((1, 6, d), lambda i: (jnp.where(i < nct, 1, 0), 0, 0)),
            pl.BlockSpec((d, _C_END), const),
            pl.BlockSpec((tm, LANES), row),
            pl.BlockSpec((tm, LANES), row),
            pl.BlockSpec((tm, LANES), row),
            pl.BlockSpec((tm, LANES), row),
            pl.BlockSpec((2, GQA_WIDTH), const),
            pl.BlockSpec((2, GQA_KV_WIDTH), const),
            pl.BlockSpec((GQA_WIDTH, GQA_WIDTH), const),
            pl.BlockSpec((GQA_KV_WIDTH, GQA_KV_WIDTH), const),
        ],
        out_specs=[pl.BlockSpec((tm, w), row) for w in widths],
        out_shape=[jax.ShapeDtypeStruct((r, w), BF16) for w in widths],
        compiler_params=_cparams(("arbitrary",)),
    )(xall, g1.reshape(1, d), mods, w_ext, cd, sd, cg, sg, gq2, gk2, bdq, bdk)


def _rope_tables(n_ctx, seq):
    rows = seq // GRID_W
    row_ids = jnp.repeat(jnp.arange(rows, dtype=F32), GRID_W)
    col_ids = jnp.tile(jnp.arange(GRID_W, dtype=F32), rows)

    def tab(rot_dim):
        n = rot_dim // 4
        inv = ROPE_THETA ** (-jnp.arange(n, dtype=F32) / n)
        ang = jnp.concatenate([row_ids[:, None] * inv, col_ids[:, None] * inv], axis=-1)
        cos, sin = jnp.cos(ang), jnp.sin(ang)
        reps = LANES // rot_dim
        c = jnp.tile(jnp.concatenate([cos, cos], axis=-1), (1, reps))
        s = jnp.tile(jnp.concatenate([-sin, sin], axis=-1), (1, reps))
        c = jnp.concatenate([jnp.ones((n_ctx, LANES), F32), c], axis=0)
        s = jnp.concatenate([jnp.zeros((n_ctx, LANES), F32), s], axis=0)
        return c, s

    cd, sd = tab(DIFF_QK_DIM)
    cg, sg = tab(HEAD_DIM)
    return cd, sd, cg, sg


def _block_diag_mean(width, chunk):
    a = np.kron(np.eye(width // chunk), np.full((chunk, chunk), 1.0 / chunk))
    return jnp.asarray(a, dtype=BF16)


VT_ROWS = 80


def _attn_kernel(*refs, diff, tq, ks, spc, nk, lam_scale):
    groups = 2 if diff else GQA_GROUP
    n_in = 5 if diff else 3
    tk = spc * ks
    o_ref = refs[n_in]
    sc = list(refs[n_in + 1:])

    def take(n):
        out = sc[:n]
        del sc[:n]
        return out

    qs_sc, m_sc, acc_sc = take(groups), take(groups), take(groups)
    alpha_sc = [take(groups), take(groups)]
    s_sc = [take(groups), take(groups)]
    pt_sc = [take(groups), take(groups)]
    if diff:
        q_ref, k_ref, vt_ref, lamp_ref, gsub_ref = refs[:n_in]
        q = q_ref[0]
        lane = lax.broadcasted_iota(jnp.int32, q.shape, 1)
        zero = jnp.zeros_like(q)
        qs_sc[0][...] = jnp.where(lane < DIFF_QK_DIM, q, zero)
        qs_sc[1][...] = jnp.where(lane >= DIFF_QK_DIM, q, zero)
    else:
        q_ref, k_ref, vt_ref = refs[:n_in]
        for g in range(groups):
            qs_sc[g][...] = q_ref[0, g]
    for g in range(groups):
        m_sc[g][...] = jnp.full(m_sc[g].shape, -jnp.inf, F32)
        acc_sc[g][...] = jnp.zeros(acc_sc[g].shape, F32)

    def scores(j, slot):
        start = j * tk if isinstance(j, int) else pl.multiple_of(j * tk, tk)
        kc = k_ref[0, pl.ds(start, tk), :]
        for g in range(groups):
            s_sc[slot][g][...] = lax.dot_general(kc, qs_sc[g][...], (((1,), (1,)), ((), ())),
                                                 preferred_element_type=F32)

    def softmax(slot):
        for g in range(groups):
            st = s_sc[slot][g][...]
            m_old = m_sc[g][...]
            m_new = jnp.maximum(m_old, jnp.max(st, axis=0, keepdims=True))
            alpha_sc[slot][g][...] = jnp.exp2(m_old - m_new)
            pt_sc[slot][g][...] = jnp.exp2(st - m_new).astype(BF16)
            m_sc[g][...] = m_new

    def values(j, slot):
        for g in range(groups):
            pv = None
            for a in range(spc):
                d = jnp.dot(vt_ref[0, j * spc + a], pt_sc[slot][g][a * ks:(a + 1) * ks, :],
                            preferred_element_type=F32)
                pv = d if pv is None else pv + d
            acc_sc[g][...] = acc_sc[g][...] * alpha_sc[slot][g][...] + pv

    def step(j, par):
        values(j - 1, 1 - par)
        scores(j + 1, 1 - par)
        softmax(par)

    scores(0, 0)
    softmax(0)
    if nk > 1:
        scores(1, 1)
        n_mid = nk - 2

        def pair(i, carry):
            j = 1 + 2 * i
            step(j, 1)
            step(j + 1, 0)
            return carry

        if n_mid // 2:
            lax.fori_loop(0, n_mid // 2, pair, 0)
        if n_mid % 2:
            step(nk - 2, (nk - 2) % 2)
        values(nk - 2, (nk - 2) % 2)
        softmax((nk - 1) % 2)
    values(nk - 1, (nk - 1) % 2)

    outs = []
    for g in range(groups):
        acc = acc_sc[g][...]
        outs.append(acc[:HEAD_DIM] / acc[HEAD_DIM:HEAD_DIM + 1])
    if diff:
        lp = lamp_ref[...]
        lam = (jnp.exp(jnp.sum(lp[0:1] * lp[1:2], axis=1, keepdims=True))
               - jnp.exp(jnp.sum(lp[2:3] * lp[3:4], axis=1, keepdims=True)) + (1.0 - lam_scale))
        od = outs[0] - lam * outs[1]
        ms = jnp.mean(od * od, axis=0, keepdims=True)
        o_ref[0] = (od * lax.rsqrt(ms + SUBLN_EPS) * gsub_ref[...] * lam_scale).astype(o_ref.dtype)
    else:
        for g in range(groups):
            o_ref[0, g] = outs[g].astype(o_ref.dtype)


def _attn_call(q, k, v, *, diff, q_row0, n_q, n_k, lamp=None, gsub=None, lam_scale=1.0):
    nh = q.shape[0]
    tq = _pick(n_q, (256, 128), q_row0)
    qb0 = q_row0 // tq
    ks = _pick(n_k, (256, 128))
    n_sub = n_k // ks
    spc = _pick(n_sub, tuple(range(ATTN_TK // ks, 0, -1)))
    nk = n_sub // spc
    tk = spc * ks
    groups = 2 if diff else GQA_GROUP
    kernel = functools.partial(_attn_kernel, diff=diff, tq=tq, ks=ks, spc=spc, nk=nk,
                               lam_scale=lam_scale)
    vk = v[:, :n_k]
    vt = jnp.concatenate([vk, jnp.ones(vk.shape[:2] + (1,), vk.dtype),
                          jnp.zeros(vk.shape[:2] + (VT_ROWS - HEAD_DIM - 1,), vk.dtype)], axis=-1)
    vt = jnp.transpose(vt.reshape(vk.shape[0], n_sub, ks, VT_ROWS), (0, 1, 3, 2))
    kv_specs = [
        pl.BlockSpec((1, n_k, HEAD_DIM), lambda h, i: (h, 0, 0)),
        pl.BlockSpec((1, n_sub, VT_ROWS, ks), lambda h, i: (h, 0, 0, 0)),
    ]
    if diff:
        in_specs = [pl.BlockSpec((1, tq, HEAD_DIM), lambda h, i: (h, i + qb0, 0))] + kv_specs + [
            pl.BlockSpec((4, DIFF_QK_DIM), lambda h, i: (0, 0)),
            pl.BlockSpec((HEAD_DIM, 1), lambda h, i: (0, 0)),
        ]
        args = (q, k, vt, lamp, gsub.reshape(HEAD_DIM, 1))
        out_spec = pl.BlockSpec((1, HEAD_DIM, tq), lambda h, i: (h, 0, i))
        out_shape = jax.ShapeDtypeStruct((nh, HEAD_DIM, n_q), BF16)
    else:
        in_specs = [pl.BlockSpec((1, groups, tq, HEAD_DIM), lambda h, i: (h, 0, i + qb0, 0))] + kv_specs
        args = (q, k, vt)
        out_spec = pl.BlockSpec((1, groups, HEAD_DIM, tq), lambda h, i: (h, 0, 0, i))
        out_shape = jax.ShapeDtypeStruct((nh, groups, HEAD_DIM, n_q), BF16)
    return pl.pallas_call(
        kernel,
        grid=(nh, n_q // tq),
        in_specs=in_specs,
        out_specs=out_spec,
        out_shape=out_shape,
        scratch_shapes=([pltpu.VMEM((tq, HEAD_DIM), BF16)] * groups
                        + [pltpu.VMEM((1, tq), F32)] * groups
                        + [pltpu.VMEM((VT_ROWS, tq), F32)] * groups
                        + [pltpu.VMEM((1, tq), F32)] * (2 * groups)
                        + [pltpu.VMEM((tk, tq), F32)] * (2 * groups)
                        + [pltpu.VMEM((tk, tq), BF16)] * (2 * groups)),
        compiler_params=_cparams(("arbitrary", "arbitrary")),
    )(*args)


def _dft_stage1_kernel(f1_ref, u_ref, z_ref):
    z = jnp.dot(f1_ref[...], u_ref[...], preferred_element_type=F32)
    l1 = u_ref.shape[0]
    z_ref[0] = z[:l1].astype(z_ref.dtype)
    z_ref[1] = z[l1:].astype(z_ref.dtype)


def _dft_stage1_call(f1s, u2d):
    l1, n = u2d.shape
    tn = _pick(n, (2048, 1024, 512, 256, 128))
    return pl.pallas_call(
        _dft_stage1_kernel,
        grid=(n // tn,),
        in_specs=[pl.BlockSpec((2 * l1, l1), lambda j: (0, 0)),
                  pl.BlockSpec((l1, tn), lambda j: (0, j))],
        out_specs=pl.BlockSpec((2, l1, tn), lambda j: (0, 0, j)),
        out_shape=jax.ShapeDtypeStruct((2, l1, n), BF16),
        compiler_params=_cparams(("arbitrary",)),
    )(f1s, u2d)


def _dft_stage2_kernel(e_ref, z_ref, cs_ref, w_ref, o_ref, *, kb, scale, real_only):
    for b in range(kb):
        zr = z_ref[0, b]
        if real_only:
            e = e_ref[b]
            x = jnp.dot(e, zr, preferred_element_type=F32)
            l2 = zr.shape[0]
            xcat = jnp.concatenate([x[:l2], x[l2:]], axis=1)
        else:
            zi = z_ref[1, b]
            rhs = jnp.concatenate(
                [jnp.concatenate([zr, zi], axis=1), jnp.concatenate([zi, -zr], axis=1)], axis=0)
            xcat = jnp.dot(e_ref[b], rhs, preferred_element_type=F32)
        y1 = jnp.dot(xcat.astype(BF16), cs_ref[...], preferred_element_type=F32)
        y = jnp.dot(y1.astype(BF16), w_ref[...], preferred_element_type=F32)
        o_ref[b] = (y * scale).astype(o_ref.dtype)


def _dft_stage2_call(e2, z, cs, wbd, *, scale, real_only):
    nz, l1, l2, cw = z.shape
    kb = _pick(l1, (8, 4, 2, 1))
    kernel = functools.partial(_dft_stage2_kernel, kb=kb, scale=scale, real_only=real_only)
    return pl.pallas_call(
        kernel,
        grid=(l1 // kb,),
        in_specs=[pl.BlockSpec((kb,) + e2.shape[1:], lambda i: (i, 0, 0)),
                  pl.BlockSpec((nz, kb, l2, cw), lambda i: (0, i, 0, 0)),
                  pl.BlockSpec(cs.shape, lambda i: (0, 0)),
                  pl.BlockSpec(wbd.shape, lambda i: (0, 0))],
        out_specs=pl.BlockSpec((kb, l2, cw), lambda i: (i, 0, 0)),
        out_shape=jax.ShapeDtypeStruct((l1, l2, cw), BF16),
        compiler_params=_cparams(("arbitrary",)),
    )(e2, z, cs, wbd)


@functools.lru_cache(maxsize=None)
def _dft_tables(n):
    l2 = FFT_L2 if n > 2 * FFT_L2 and n % FFT_L2 == 0 else n
    l1 = n // l2
    k1 = np.arange(l1)
    ang1 = 2.0 * np.pi * ((k1[:, None] * k1[None, :]) % l1) / l1
    f1s = np.concatenate([np.cos(ang1), -np.sin(ang1)], axis=0)
    k = k1[:, None] + l1 * np.arange(l2)[None, :]
    n2 = np.arange(l2)
    ang = 2.0 * np.pi * ((k[:, :, None] * n2[None, None, :]) % n) / n
    if l1 == 1:
        e2 = np.concatenate([np.cos(ang), -np.sin(ang)], axis=1)
    else:
        e2 = np.concatenate([np.cos(ang), np.sin(ang)], axis=2)
    c = np.arange(FOURIER_GROUP_DIM)
    angc = 2.0 * np.pi * ((c[:, None] * c[None, :]) % FOURIER_GROUP_DIM) / FOURIER_GROUP_DIM
    eye = np.eye(N_FOURIER_GROUPS)
    cs = np.concatenate([np.kron(eye, np.cos(angc)), np.kron(eye, np.sin(angc))], axis=0)
    return l1, l2, f1s.astype(np.float32), e2.astype(np.float32), cs.astype(np.float32)


def _fnet_call(u, wbd):
    n, cw = u.shape
    l1, l2, f1s, e2, cs = _dft_tables(n)
    scale = 1.0 / math.sqrt(n * FOURIER_GROUP_DIM)
    e2 = jnp.asarray(e2, dtype=BF16)
    cs = jnp.asarray(cs, dtype=BF16)
    if l1 == 1:
        z = u.reshape(1, 1, l2, cw)
        y = _dft_stage2_call(e2, z, cs, wbd, scale=scale, real_only=True)
        return y.reshape(n, cw)
    z = _dft_stage1_call(jnp.asarray(f1s, dtype=BF16), u.reshape(l1, l2 * cw))
    y = _dft_stage2_call(e2, z.reshape(2, l1, l2, cw), cs, wbd, scale=scale, real_only=False)
    return jnp.transpose(y, (1, 0, 2)).reshape(n, cw)


def _store_token_tiles(ref, x):
    n, d = x.shape
    for s in range(d // LANES):
        ref[pl.ds(s, n, stride=SUBLANES), :] = x[:, s * LANES:(s + 1) * LANES]


def _load_token_tiles(ref, n):
    parts = [ref[pl.ds(s, n, stride=SUBLANES), :] for s in range(SUBLANES)]
    return jnp.concatenate(parts, axis=1)


def _out_kernel(mix_ref, x_ref, w_ref, m_ref, g2_ref, wr_ref, br_ref,
                xo_ref, h2_ref, gmat_ref, g4_ref, i4_ref):
    o = jnp.dot(mix_ref[...], w_ref[...], preferred_element_type=F32)
    xn = x_ref[...] + m_ref[0, 2:3, :] * o
    xo_ref[...] = xn
    ms = jnp.mean(xn * xn, axis=-1, keepdims=True)
    h2 = xn * lax.rsqrt(ms + NORM_EPS) * g2_ref[...]
    h2 = h2 * (1.0 + m_ref[0, 4:5, :]) + m_ref[0, 3:4, :]
    _store_token_tiles(h2_ref, h2)
    logits = jnp.dot(h2, wr_ref[...], precision=lax.Precision.HIGHEST,
                     preferred_element_type=F32) + br_ref[...]
    iota = lax.broadcasted_iota(jnp.int32, logits.shape, 1).astype(F32)
    vals, idxs = [], []
    l = logits
    for _ in range(TOP_K):
        mx = jnp.max(l, axis=1, keepdims=True)
        ik = jnp.min(jnp.where(l == mx, iota, float(ROUTER_PAD)), axis=1, keepdims=True)
        vals.append(mx)
        idxs.append(ik)
        l = jnp.where(iota == ik, -jnp.inf, l)
    es = [jnp.exp(v - vals[0]) for v in vals]
    den = es[0] + es[1] + es[2] + es[3]
    gs = [e / den for e in es]
    gmat = jnp.zeros(logits.shape, F32)
    for ik, g in zip(idxs, gs):
        gmat = jnp.where(iota == ik, g, gmat)
    gmat_ref[...] = gmat
    lane4 = lax.broadcasted_iota(jnp.int32, g4_ref.shape, 1)
    g4 = jnp.zeros(g4_ref.shape, F32)
    i4 = jnp.zeros(g4_ref.shape, F32)
    for k in range(TOP_K):
        g4 = jnp.where(lane4 == k, gs[k], g4)
        i4 = jnp.where(lane4 == k, idxs[k], i4)
    g4_ref[...] = g4
    i4_ref[...] = i4.astype(jnp.int32)


def _out_call(mix, xsrc, x_row0, w_out, mods, g2, wr, br, n_ctx_rows):
    n, d = mix.shape
    tm = _pick(n, (256, 128, 64, 32, 16), x_row0, n_ctx_rows)
    xb0 = x_row0 // tm
    nct = n_ctx_rows // tm
    row = lambda i: (i, 0)
    const = lambda i: (0, 0)
    return pl.pallas_call(
        _out_kernel,
        grid=(n // tm,),
        in_specs=[
            pl.BlockSpec((tm, d), row),
            pl.BlockSpec((tm, d), lambda i: (i + xb0, 0)),
            pl.BlockSpec((d, d), const),
            pl.BlockSpec((1, 6, d), lambda i: (jnp.where(i < nct, 1, 0), 0, 0)),
            pl.BlockSpec((1, d), const),
            pl.BlockSpec((d, ROUTER_PAD), const),
            pl.BlockSpec((1, ROUTER_PAD), const),
        ],
        out_specs=[
            pl.BlockSpec((tm, d), row),
            pl.BlockSpec((tm * SUBLANES, LANES), row),
            pl.BlockSpec((tm, ROUTER_PAD), row),
            pl.BlockSpec((tm, TOP_K), row),
            pl.BlockSpec((tm, TOP_K), row),
        ],
        out_shape=[
            jax.ShapeDtypeStruct((n, d), F32),
            jax.ShapeDtypeStruct((n * SUBLANES, LANES), F32),
            jax.ShapeDtypeStruct((n, ROUTER_PAD), F32),
            jax.ShapeDtypeStruct((n, TOP_K), F32),
            jax.ShapeDtypeStruct((n, TOP_K), jnp.int32),
        ],
        compiler_params=_cparams(("arbitrary",)),
    )(mix, xsrc, w_out, mods, g2.reshape(1, d), wr, br)


def _rank_kernel(g_ref, rank_ref, cnt_ref, carry_sc):
    @pl.when(pl.program_id(0) == 0)
    def _():
        carry_sc[...] = jnp.zeros(carry_sc.shape, F32)

    self32 = jnp.where(g_ref[...] > 0.0, 1.0, 0.0)
    sel = self32.astype(BF16)
    tm = sel.shape[0]
    r = lax.broadcasted_iota(jnp.int32, (tm, tm), 0)
    c = lax.broadcasted_iota(jnp.int32, (tm, tm), 1)
    tri = jnp.where(c < r, 1.0, 0.0).astype(BF16)
    excl = jnp.dot(tri, sel, preferred_element_type=F32) + carry_sc[0:1, :]
    rank_ref[...] = excl.astype(jnp.int32)
    tot = carry_sc[0:1, :] + jnp.sum(self32, axis=0, keepdims=True)
    carry_sc[...] = jnp.broadcast_to(tot, carry_sc.shape)
    cnt_ref[...] = jnp.broadcast_to(tot, cnt_ref.shape).astype(jnp.int32)


def _rank_call(gmat):
    n, w = gmat.shape
    tm = _pick(n, (256, 128, 64, 32, 16))
    return pl.pallas_call(
        _rank_kernel,
        grid=(n // tm,),
        in_specs=[pl.BlockSpec((tm, w), lambda i: (i, 0))],
        out_specs=[pl.BlockSpec((tm, w), lambda i: (i, 0)),
                   pl.BlockSpec((SUBLANES, w), lambda i: (0, 0))],
        out_shape=[jax.ShapeDtypeStruct((n, w), jnp.int32),
                   jax.ShapeDtypeStruct((SUBLANES, w), jnp.int32)],
        scratch_shapes=[pltpu.VMEM((SUBLANES, w), F32)],
        compiler_params=_cparams(("arbitrary",)),
    )(gmat)


def _dispatch_kernel(dest_ref, h_ref, xs_in_ref, xs_ref, sem, *, tt):
    del xs_in_ref
    base = pl.program_id(0) * tt

    def copy(p):
        src = pl.multiple_of((base + p // TOP_K) * SUBLANES, SUBLANES)
        dst = pl.multiple_of(dest_ref[base * TOP_K + p] * SUBLANES, SUBLANES)
        return pltpu.make_async_copy(h_ref.at[pl.ds(src, SUBLANES)],
                                     xs_ref.at[pl.ds(dst, SUBLANES)], sem)

    def issue(p, c):
        copy(p).start()
        return c

    def drain(p, c):
        copy(p).wait()
        return c

    lax.fori_loop(0, tt * TOP_K, issue, 0)
    lax.fori_loop(0, tt * TOP_K, drain, 0)


def _dispatch_call(dest_flat, h2t, n_slots):
    n = h2t.shape[0] // SUBLANES
    tt = _pick(n, (512, 256, 128, 64, 32, 16))
    xs0 = jnp.zeros((n_slots * SUBLANES, LANES), F32)
    return pl.pallas_call(
        functools.partial(_dispatch_kernel, tt=tt),
        grid_spec=pltpu.PrefetchScalarGridSpec(
            num_scalar_prefetch=1,
            grid=(n // tt,),
            in_specs=[pl.BlockSpec(memory_space=pl.ANY), pl.BlockSpec(memory_space=pl.ANY)],
            out_specs=pl.BlockSpec(memory_space=pl.ANY),
            scratch_shapes=[pltpu.SemaphoreType.DMA(())],
        ),
        out_shape=jax.ShapeDtypeStruct(xs0.shape, F32),
        input_output_aliases={2: 0},
        compiler_params=_cparams(("arbitrary",)),
    )(dest_flat, h2t, xs0)


def _expert_kernel(be_ref, nb_ref, x_ref, wgu_ref, bgu_ref, wd_ref, bd_ref, y_ref, wgu_sc, wd_sc):
    b = pl.program_id(0)
    active = b < nb_ref[0]

    @pl.when(jnp.logical_and(active, jnp.logical_or(b == 0, be_ref[b] != be_ref[jnp.maximum(b - 1, 0)])))
    def _():
        wgu_sc[...] = wgu_ref[0].astype(BF16)
        wd_sc[...] = wd_ref[0].astype(BF16)

    @pl.when(active)
    def _():
        x = _load_token_tiles(x_ref, MOE_BLOCK).astype(BF16)
        gu = jnp.dot(x, wgu_sc[...], preferred_element_type=F32) + bgu_ref[0]
        glu = jnp.minimum(gu[:, :EXPERT_FF], SWIGLU_LIMIT)
        lin = jnp.clip(gu[:, EXPERT_FF:], -SWIGLU_LIMIT, SWIGLU_LIMIT)
        act = glu * jax.nn.sigmoid(SWIGLU_ALPHA * glu) * (lin + 1.0)
        y = jnp.dot(act.astype(BF16), wd_sc[...], preferred_element_type=F32) + bd_ref[0]
        _store_token_tiles(y_ref, y)

    @pl.when(pl.program_id(0) >= nb_ref[0])
    def _():
        y_ref[...] = jnp.zeros(y_ref.shape, F32)


def _expert_call(blk_e, nb_used, xs, wgu, bgu, wd, bd):
    nb = xs.shape[0] // (MOE_BLOCK * SUBLANES)
    ne, d, ff2 = wgu.shape
    blk = lambda b, be, nbu: (jnp.minimum(b, nbu[0] - 1), 0)
    wsel = lambda b, be, nbu: (be[jnp.minimum(b, nbu[0] - 1)], 0, 0)
    return pl.pallas_call(
        _expert_kernel,
        grid_spec=pltpu.PrefetchScalarGridSpec(
            num_scalar_prefetch=2,
            grid=(nb,),
            in_specs=[
                pl.BlockSpec((MOE_BLOCK * SUBLANES, LANES), blk),
                pl.BlockSpec((1, d, ff2), wsel),
                pl.BlockSpec((1, 1, ff2), wsel),
                pl.BlockSpec((1, ff2 // 2, d), wsel),
                pl.BlockSpec((1, 1, d), wsel),
            ],
            out_specs=pl.BlockSpec((MOE_BLOCK * SUBLANES, LANES), lambda b, be, nbu: (b, 0)),
            scratch_shapes=[pltpu.VMEM((d, ff2), BF16), pltpu.VMEM((ff2 // 2, d), BF16)],
        ),
        out_shape=jax.ShapeDtypeStruct(xs.shape, F32),
        compiler_params=_cparams(("arbitrary",)),
    )(blk_e, nb_used, xs, wgu, bgu.reshape(ne, 1, ff2), wd, bd.reshape(ne, 1, d))


def _combine_kernel(dest_ref, ys_ref, x_ref, g4_ref, m_ref, gf_ref, o_ref, buf, sem, *, tm, final):
    base = pl.program_id(0) * tm * TOP_K

    def copy(p):
        src = pl.multiple_of(dest_ref[base + p] * SUBLANES, SUBLANES)
        dst = pl.multiple_of((p // TOP_K) * SUBLANES, SUBLANES)
        return pltpu.make_async_copy(ys_ref.at[pl.ds(src, SUBLANES)],
                                     buf.at[p % TOP_K, pl.ds(dst, SUBLANES)], sem)

    def issue(p, c):
        copy(p).start()
        return c

    def drain(p, c):
        copy(p).wait()
        return c

    lax.fori_loop(0, tm * TOP_K, issue, 0)
    lax.fori_loop(0, tm * TOP_K, drain, 0)

    g4 = g4_ref[...]
    y = _load_token_tiles(buf.at[0], tm) * g4[:, 0:1]
    for k in range(1, TOP_K):
        y = y + _load_token_tiles(buf.at[k], tm) * g4[:, k:k + 1]
    xo = x_ref[...] + m_ref[0, 5:6, :] * y
    if final:
        ms = jnp.mean(xo * xo, axis=-1, keepdims=True)
        xo = xo * lax.rsqrt(ms + NORM_EPS) * gf_ref[...]
    o_ref[...] = xo


def _combine_call(dest_flat, ys, x, g4, mods, g_final, n_ctx_rows, final):
    n, d = x.shape
    tm = _pick(n, (256, 128, 64, 32, 16), n_ctx_rows)
    nct = n_ctx_rows // tm
    row = lambda i, dref: (i, 0)
    return pl.pallas_call(
        functools.partial(_combine_kernel, tm=tm, final=final),
        grid_spec=pltpu.PrefetchScalarGridSpec(
            num_scalar_prefetch=1,
            grid=(n // tm,),
            in_specs=[
                pl.BlockSpec(memory_space=pl.ANY),
                pl.BlockSpec((tm, d), row),
                pl.BlockSpec((tm, TOP_K), row),
                pl.BlockSpec((1, 6, d), lambda i, dref: (jnp.where(i < nct, 1, 0), 0, 0)),
                pl.BlockSpec((1, d), lambda i, dref: (0, 0)),
            ],
            out_specs=pl.BlockSpec((tm, d), row),
            scratch_shapes=[pltpu.VMEM((TOP_K, tm * SUBLANES, LANES), F32),
                            pltpu.SemaphoreType.DMA(())],
        ),
        out_shape=jax.ShapeDtypeStruct((n, d), F32),
        compiler_params=_cparams(("arbitrary",)),
    )(dest_flat, ys, x, g4, mods, g_final.reshape(1, d))


def _moe(x_mid, h2t, gmat, g4, i4, mods, n_ctx_rows, wgu, bgu, wd, bd, g_final, final):
    n, d = x_mid.shape
    assert d == SUBLANES * LANES
    rank, cnt = _rank_call(gmat)
    counts = cnt[0, :N_EXPERTS]
    padded = (counts + MOE_BLOCK - 1) // MOE_BLOCK * MOE_BLOCK
    ends = jnp.cumsum(padded)
    starts = ends - padded
    dest = starts[i4] + jnp.take_along_axis(rank, i4, axis=1)
    dest_flat = dest.reshape(-1).astype(jnp.int32)
    n_slots = (n * TOP_K + MOE_BLOCK - 1) // MOE_BLOCK * MOE_BLOCK + N_EXPERTS * MOE_BLOCK
    nb = n_slots // MOE_BLOCK
    blk_start = jnp.arange(nb, dtype=jnp.int32) * MOE_BLOCK
    blk_e = jnp.minimum(jnp.sum((ends[None, :] <= blk_start[:, None]).astype(jnp.int32), axis=1),
                        N_EXPERTS - 1).astype(jnp.int32)
    nb_used = (ends[-1] // MOE_BLOCK).astype(jnp.int32).reshape(1)
    xs = _dispatch_call(dest_flat, h2t, n_slots)
    ys = _expert_call(blk_e, nb_used, xs, wgu, bgu, wd, bd)
    return _combine_call(dest_flat, ys, x_mid, g4, mods, g_final, n_ctx_rows, final)


def _heads(a, nh):
    r = a.shape[0]
    return jnp.transpose(a.reshape(r, nh, HEAD_DIM), (1, 0, 2))


def _unheads(o):
    n = o.shape[-1]
    o = o.reshape(-1, HEAD_DIM, n)
    return jnp.transpose(o, (2, 0, 1)).reshape(n, -1)


def kernel(x, c, ctx, c_ctx, w_mod, b_mod, g_norm1, g_norm2, w_in, w_out, w_fnet, lambda_q1, lambda_k1, lambda_q2, lambda_k2, g_subln, g_qnorm, g_knorm, w_router, b_router, w_gate_up, b_gate_up, w_down, b_down, g_final):
    bsz, seq, d = x.shape
    assert bsz == 1
    n_ctx = ctx.shape[1]
    depth = w_mod.shape[0]
    r = n_ctx + seq

    cc = jnp.zeros((SUBLANES, d), F32).at[0].set(c[0]).at[1].set(c_ctx)
    mod_all = _mod_call(cc, w_mod, b_mod)
    tabs = _rope_tables(n_ctx, seq)
    bdq = _block_diag_mean(GQA_WIDTH, HEAD_DIM)
    bdk = _block_diag_mean(GQA_KV_WIDTH, HEAD_DIM)

    xall = jnp.concatenate([ctx[0], x[0]], axis=0)
    out = None
    for l in range(depth):
        need_ctx = l < depth - 1
        lam_init = 0.8 - 0.6 * math.exp(-0.3 * l)
        mods = mod_all[l, :2].reshape(2, 6, d)
        wl = w_in[l]
        w_ext = jnp.concatenate([
            wl,
            _swap_halves_cols(wl[:, _C_DQ:_C_DK], DIFF_QK_DIM),
            _swap_halves_cols(wl[:, _C_DK:_C_DV], DIFF_QK_DIM),
            _swap_halves_cols(wl[:, _C_GQ:_C_GK], HEAD_DIM),
            _swap_halves_cols(wl[:, _C_GK:_C_GV], HEAD_DIM),
        ], axis=1).astype(BF16)
        gq = g_qnorm[l]
        gk = g_knorm[l]
        gq_sw = jnp.concatenate([gq[HEAD_DIM // 2:], gq[:HEAD_DIM // 2]])
        gk_sw = jnp.concatenate([gk[HEAD_DIM // 2:], gk[:HEAD_DIM // 2]])
        gq2 = jnp.stack([jnp.tile(gq, N_GQA_Q_HEADS), jnp.tile(gq_sw, N_GQA_Q_HEADS)])
        gk2 = jnp.stack([jnp.tile(gk, N_GQA_KV_HEADS), jnp.tile(gk_sw, N_GQA_KV_HEADS)])

        f, dq, dk, dv, gqo, gko, gvo = _in_call(xall, g_norm1[l], mods, w_ext, tabs, gq2, gk2,
                                                 bdq, bdk, n_ctx)

        dqh = _heads(dq, N_DIFF_HEADS)
        dkh = _heads(dk, N_DIFF_HEADS)
        dvh = _heads(dv, N_DIFF_HEADS)
        gqh = _heads(gqo, N_GQA_Q_HEADS).reshape(N_GQA_KV_HEADS, GQA_GROUP, r, HEAD_DIM)
        gkh = _heads(gko, N_GQA_KV_HEADS)
        gvh = _heads(gvo, N_GQA_KV_HEADS)
        lamp = jnp.stack([lambda_q1[l], lambda_k1[l], lambda_q2[l], lambda_k2[l]]).astype(F32)
        gsub = g_subln[l].reshape(1, HEAD_DIM)
        wbd = jnp.zeros((FOURIER_WIDTH, FOURIER_WIDTH), F32)
        for g in range(N_FOURIER_GROUPS):
            sl = slice(g * FOURIER_GROUP_DIM, (g + 1) * FOURIER_GROUP_DIM)
            wbd = wbd.at[sl, sl].set(w_fnet[l, g])
        wbd = wbd.astype(BF16)

        def mixers(q_row0, n_q, n_k):
            d_o = _attn_call(dqh, dkh, dvh, diff=True, q_row0=q_row0, n_q=n_q, n_k=n_k,
                             lamp=lamp, gsub=gsub, lam_scale=1.0 - lam_init)
            g_o = _attn_call(gqh, gkh, gvh, diff=False, q_row0=q_row0, n_q=n_q, n_k=n_k)
            f_o = _fnet_call(f[q_row0:q_row0 + n_q], wbd)
            return jnp.concatenate([f_o, _unheads(d_o), _unheads(g_o)], axis=1)

        mix = mixers(n_ctx, seq, r)
        if need_ctx:
            mix = jnp.concatenate([mixers(0, n_ctx, n_ctx), mix], axis=0)
            x_row0, nct = 0, n_ctx
        else:
            x_row0, nct = n_ctx, 0

        wr = jnp.zeros((d, ROUTER_PAD), F32).at[:, :N_EXPERTS].set(w_router[l])
        br = jnp.full((1, ROUTER_PAD), NEG_BIG, F32).at[0, :N_EXPERTS].set(b_router[l])
        x_mid, h2, gmat, g4, i4 = _out_call(mix, xall, x_row0, w_out[l].astype(BF16), mods,
                                            g_norm2[l], wr, br, nct)
        xall = _moe(x_mid, h2, gmat, g4, i4, mods, nct,
                    w_gate_up[l], b_gate_up[l], w_down[l], b_down[l],
                    g_final, final=not need_ctx)
    return xall.reshape(1, seq, d)
```

```python
import functools
import math

import numpy as np
import jax
import jax.numpy as jnp
from jax import lax
from jax.experimental import pallas as pl
from jax.experimental.pallas import tpu as pltpu

F32 = jnp.float32
BF16 = jnp.bfloat16

GRID_W = 64
HEAD_DIM = 64
N_FOURIER_GROUPS = 4
FOURIER_GROUP_DIM = 64
FOURIER_WIDTH = N_FOURIER_GROUPS * FOURIER_GROUP_DIM
N_DIFF_HEADS = 6
DIFF_QK_DIM = HEAD_DIM // 2
DIFF_WIDTH = N_DIFF_HEADS * HEAD_DIM
N_GQA_Q_HEADS = 6
N_GQA_KV_HEADS = 2
GQA_GROUP = N_GQA_Q_HEADS // N_GQA_KV_HEADS
GQA_WIDTH = N_GQA_Q_HEADS * HEAD_DIM
GQA_KV_WIDTH = N_GQA_KV_HEADS * HEAD_DIM
N_EXPERTS = 32
TOP_K = 4
EXPERT_FF = 1024
SWIGLU_ALPHA = 1.702
SWIGLU_LIMIT = 7.0
ROPE_THETA = 10000.0
NORM_EPS = 1e-6
SUBLN_EPS = 1e-5
LOG2E = math.log2(math.e)
DIFF_SCALE = DIFF_QK_DIM ** -0.5 * LOG2E
GQA_SCALE = HEAD_DIM ** -0.5 * LOG2E

LANES = 128
SUBLANES = 8
VMEM_LIMIT = 56 * 1024 * 1024

ROUTER_PAD = LANES
NEG_BIG = -1e30
MOE_BLOCK = 256
ATTN_TK = 1280
ATTN_SLOTS = 3
FFT_L2 = 128


def _pick(n, prefs, *offsets):
    g = n
    for o in offsets:
        g = math.gcd(g, o)
    for p in prefs:
        if g % p == 0:
            return p
    return g


def _cparams(sem):
    return pltpu.CompilerParams(dimension_semantics=sem, vmem_limit_bytes=VMEM_LIMIT)


def _mod_kernel(c_ref, w_ref, b_ref, o_ref):
    c = c_ref[...]
    s = c * jax.nn.sigmoid(c)
    o_ref[0] = jnp.dot(s, w_ref[0], precision=lax.Precision.HIGHEST,
                       preferred_element_type=F32) + b_ref[0]


def _mod_call(cc, w_mod, b_mod):
    depth, d, n6 = w_mod.shape
    tn = _pick(n6, (1536, 1024, 512, 256, 128))
    return pl.pallas_call(
        _mod_kernel,
        grid=(depth, n6 // tn),
        in_specs=[
            pl.BlockSpec((SUBLANES, d), lambda l, j: (0, 0)),
            pl.BlockSpec((1, d, tn), lambda l, j: (l, 0, j)),
            pl.BlockSpec((1, 1, tn), lambda l, j: (l, 0, j)),
        ],
        out_specs=pl.BlockSpec((1, SUBLANES, tn), lambda l, j: (l, 0, j)),
        out_shape=jax.ShapeDtypeStruct((depth, SUBLANES, n6), F32),
        compiler_params=_cparams(("arbitrary", "arbitrary")),
    )(cc, w_mod, b_mod.reshape(depth, 1, n6))


_C_F = 0
_C_DQ = _C_F + FOURIER_WIDTH
_C_DK = _C_DQ + DIFF_WIDTH
_C_DV = _C_DK + DIFF_WIDTH
_C_GQ = _C_DV + DIFF_WIDTH
_C_GK = _C_GQ + GQA_WIDTH
_C_GV = _C_GK + GQA_KV_WIDTH
_C_DQS = _C_GV + GQA_KV_WIDTH
_C_DKS = _C_DQS + DIFF_WIDTH
_C_GQS = _C_DKS + DIFF_WIDTH
_C_GKS = _C_GQS + GQA_WIDTH
_C_END = _C_GKS + GQA_KV_WIDTH


def _swap_halves_cols(w, chunk):
    k, n = w.shape
    w = w.reshape(k, n // chunk, 2, chunk // 2)
    return w[:, :, ::-1, :].reshape(k, n)


def _head_mean_sq(z, bd):
    q2 = z * z
    hi = q2.astype(BF16)
    lo = (q2 - hi.astype(F32)).astype(BF16)
    return (jnp.dot(hi, bd, preferred_element_type=F32)
            + jnp.dot(lo, bd, preferred_element_type=F32))


def _in_kernel(x_ref, g_ref, m_ref, w_ref, cd_ref, sd_ref, cg_ref, sg_ref,
               gq_ref, gk_ref, bdq_ref, bdk_ref,
               f_ref, dq_ref, dk_ref, dv_ref, gqo_ref, gko_ref, gvo_ref):
    x = x_ref[...]
    ms = jnp.mean(x * x, axis=-1, keepdims=True)
    y = x * lax.rsqrt(ms + NORM_EPS) * g_ref[...]
    h = y * (1.0 + m_ref[0, 1:2, :]) + m_ref[0, 0:1, :]
    z = jnp.dot(h.astype(BF16), w_ref[...], preferred_element_type=F32)

    f_ref[...] = z[:, _C_F:_C_DQ].astype(BF16)
    dv_ref[...] = z[:, _C_DV:_C_GQ].astype(BF16)
    gvo_ref[...] = z[:, _C_GV:_C_DQS].astype(BF16)

    cd = cd_ref[...]
    sd = sd_ref[...]
    cd3 = jnp.concatenate([cd, cd, cd], axis=1)
    sd3 = jnp.concatenate([sd, sd, sd], axis=1)
    dq = z[:, _C_DQ:_C_DK] * cd3 + z[:, _C_DQS:_C_DKS] * sd3
    dq_ref[...] = (dq * DIFF_SCALE).astype(BF16)
    dk = z[:, _C_DK:_C_DV] * cd3 + z[:, _C_DKS:_C_GQS] * sd3
    dk_ref[...] = dk.astype(BF16)

    cg = cg_ref[...]
    sg = sg_ref[...]
    cg3 = jnp.concatenate([cg, cg, cg], axis=1)
    sg3 = jnp.concatenate([sg, sg, sg], axis=1)
    zq = z[:, _C_GQ:_C_GK]
    rq = lax.rsqrt(_head_mean_sq(zq, bdq_ref[...]) + NORM_EPS)
    gq = rq * (zq * (gq_ref[0:1, :] * cg3) + z[:, _C_GQS:_C_GKS] * (gq_ref[1:2, :] * sg3))
    gqo_ref[...] = (gq * GQA_SCALE).astype(BF16)
    zk = z[:, _C_GK:_C_GV]
    rk = lax.rsqrt(_head_mean_sq(zk, bdk_ref[...]) + NORM_EPS)
    gk = rk * (zk * (gk_ref[0:1, :] * cg) + z[:, _C_GKS:_C_END] * (gk_ref[1:2, :] * sg))
    gko_ref[...] = gk.astype(BF16)


def _in_call(xall, g1, mods, w_ext, tabs, gq2, gk2, bdq, bdk, n_ctx_rows):
    r, d = xall.shape
    tm = _pick(r, (256, 128, 64, 32, 16), n_ctx_rows)
    nct = n_ctx_rows // tm
    cd, sd, cg, sg = tabs
    row = lambda i: (i, 0)
    const = lambda i: (0, 0)
    widths = (FOURIER_WIDTH, DIFF_WIDTH, DIFF_WIDTH, DIFF_WIDTH, GQA_WIDTH, GQA_KV_WIDTH, GQA_KV_WIDTH)
    return pl.pallas_call(
        _in_kernel,
        grid=(r // tm,),
        in_specs=[
            pl.BlockSpec((tm, d), row),
            pl.BlockSpec((1, d), const),
            pl.BlockSpec((1, 6, d), lambda i: (jnp.where(i < nct, 1, 0), 0, 0)),
            pl.BlockSpec((d, _C_END), const),
            pl.BlockSpec((tm, LANES), row),
            pl.BlockSpec((tm, LANES), row),
            pl.BlockSpec((tm, LANES), row),
            pl.BlockSpec((tm, LANES), row),
            pl.BlockSpec((2, GQA_WIDTH), const),
            pl.BlockSpec((2, GQA_KV_WIDTH), const),
            pl.BlockSpec((GQA_WIDTH, GQA_WIDTH), const),
            pl.BlockSpec((GQA_KV_WIDTH, GQA_KV_WIDTH), const),
        ],
        out_specs=[pl.BlockSpec((tm, w), row) for w in widths],
        out_shape=[jax.ShapeDtypeStruct((r, w), BF16) for w in widths],
        compiler_params=_cparams(("arbitrary",)),
    )(xall, g1.reshape(1, d), mods, w_ext, cd, sd, cg, sg, gq2, gk2, bdq, bdk)


def _rope_tables(n_ctx, seq):
    rows = seq // GRID_W
    row_ids = jnp.repeat(jnp.arange(rows, dtype=F32), GRID_W)
    col_ids = jnp.tile(jnp.arange(GRID_W, dtype=F32), rows)

    def tab(rot_dim):
        n = rot_dim // 4
        inv = ROPE_THETA ** (-jnp.arange(n, dtype=F32) / n)
        ang = jnp.concatenate([row_ids[:, None] * inv, col_ids[:, None] * inv], axis=-1)
        cos, sin = jnp.cos(ang), jnp.sin(ang)
        reps = LANES // rot_dim
        c = jnp.tile(jnp.concatenate([cos, cos], axis=-1), (1, reps))
        s = jnp.tile(jnp.concatenate([-sin, sin], axis=-1), (1, reps))
        c = jnp.concatenate([jnp.ones((n_ctx, LANES), F32), c], axis=0)
        s = jnp.concatenate([jnp.zeros((n_ctx, LANES), F32), s], axis=0)
        return c, s

    cd, sd = tab(DIFF_QK_DIM)
    cg, sg = tab(HEAD_DIM)
    return cd, sd, cg, sg


def _block_diag_mean(width, chunk):
    a = np.kron(np.eye(width // chunk), np.full((chunk, chunk), 1.0 / chunk))
    return jnp.asarray(a, dtype=BF16)


VT_ROWS = 80


def _attn_kernel(*refs, diff, tq, ks, spc, nk, lam_scale):
    groups = 2 if diff else GQA_GROUP
    n_in = 5 if diff else 3
    tk = spc * ks
    o_ref = refs[n_in]
    sc = list(refs[n_in + 1:])

    def take(n):
        out = sc[:n]
        del sc[:n]
        return out

    qt_sc, m_sc, acc_sc = take(groups), take(groups), take(groups)
    slots, skew = ATTN_SLOTS, ATTN_SLOTS - 1
    cmax_sc = [take(groups) for _ in range(slots)]
    s_sc = [take(groups) for _ in range(slots)]
    if diff:
        qt_ref, k_ref, vt_ref, lamp_ref, gsub_ref = refs[:n_in]
        qt = qt_ref[0]
        row = lax.broadcasted_iota(jnp.int32, qt.shape, 0)
        zero = jnp.zeros_like(qt)
        qt_sc[0][...] = jnp.where(row < DIFF_QK_DIM, qt, zero)
        qt_sc[1][...] = jnp.where(row >= DIFF_QK_DIM, qt, zero)
    else:
        qt_ref, k_ref, vt_ref = refs[:n_in]
        for g in range(groups):
            qt_sc[g][...] = qt_ref[0, g]
    for g in range(groups):
        m_sc[g][...] = jnp.full(m_sc[g].shape, -jnp.inf, F32)
        acc_sc[g][...] = jnp.zeros(acc_sc[g].shape, F32)

    def key_rows(j, a):
        start = j * tk + a * ks
        if not isinstance(start, int):
            start = pl.multiple_of(start, ks)
        return k_ref[0, pl.ds(start, ks), :]

    def step(j, r, do_values, do_scores):
        vs, ss = r % slots, (r + skew) % slots
        rows = [pl.ds(a * ks, ks) for a in range(spc)]
        if do_values:
            m_new, alpha = [], []
            for g in range(groups):
                m_old = m_sc[g][...]
                m_new.append(jnp.maximum(m_old, cmax_sc[vs][g][...]))
                alpha.append(jnp.exp2(m_old - m_new[g]))
                m_sc[g][...] = m_new[g]
        pv = [None] * groups
        cm = [None] * groups
        for a in range(spc):
            if do_scores:
                ka = key_rows(j + skew, a)
                for g in range(groups):
                    st = jnp.dot(ka, qt_sc[g][...], preferred_element_type=F32)
                    s_sc[ss][g][rows[a], :] = st
                    ca = jnp.max(st, axis=0, keepdims=True)
                    cm[g] = ca if cm[g] is None else jnp.maximum(cm[g], ca)
            if do_values:
                for g in range(groups):
                    pt = jnp.exp2(s_sc[vs][g][rows[a], :] - m_new[g]).astype(BF16)
                    d = jnp.dot(vt_ref[0, j * spc + a], pt, preferred_element_type=F32)
                    pv[g] = d if pv[g] is None else pv[g] + d
        for g in range(groups):
            if do_scores:
                cmax_sc[ss][g][...] = cm[g]
            if do_values:
                acc_sc[g][...] = acc_sc[g][...] * alpha[g] + pv[g]

    for c in range(min(skew, nk)):
        step(c - skew, c - skew, False, True)
    n_both = max(nk - skew, 0)

    def ring(i, carry):
        for r in range(slots):
            step(slots * i + r, r, True, True)
        return carry

    if n_both // slots:
        lax.fori_loop(0, n_both // slots, ring, 0)
    for j in range(n_both - n_both % slots, nk):
        step(j, j, True, j < n_both)

    outs = []
    for g in range(groups):
        acc = acc_sc[g][...]
        outs.append(acc[:HEAD_DIM] / acc[HEAD_DIM:HEAD_DIM + 1])
    if diff:
        lp = lamp_ref[...]
        lam = (jnp.exp(jnp.sum(lp[0:1] * lp[1:2], axis=1, keepdims=True))
               - jnp.exp(jnp.sum(lp[2:3] * lp[3:4], axis=1, keepdims=True)) + (1.0 - lam_scale))
        od = outs[0] - lam * outs[1]
        ms = jnp.mean(od * od, axis=0, keepdims=True)
        o_ref[0] = (od * lax.rsqrt(ms + SUBLN_EPS) * gsub_ref[...] * lam_scale).astype(o_ref.dtype)
    else:
        for g in range(groups):
            o_ref[0, g] = outs[g].astype(o_ref.dtype)


def _attn_call(qt, k, v, *, diff, q_row0, n_q, n_k, lamp=None, gsub=None, lam_scale=1.0):
    nh = qt.shape[0]
    tq = _pick(n_q, (256, 128), q_row0)
    qb0 = q_row0 // tq
    ks = _pick(n_k, (256, 128))
    n_sub = n_k // ks
    spc = _pick(n_sub, tuple(range(ATTN_TK // ks, 0, -1)))
    nk = n_sub // spc
    tk = spc * ks
    groups = 2 if diff else GQA_GROUP
    kernel = functools.partial(_attn_kernel, diff=diff, tq=tq, ks=ks, spc=spc, nk=nk,
                               lam_scale=lam_scale)
    vk = v[:, :n_k]
    vt = jnp.concatenate([vk, jnp.ones(vk.shape[:2] + (1,), vk.dtype),
                          jnp.zeros(vk.shape[:2] + (VT_ROWS - HEAD_DIM - 1,), vk.dtype)], axis=-1)
    vt = jnp.transpose(vt.reshape(vk.shape[0], n_sub, ks, VT_ROWS), (0, 1, 3, 2))
    kv_specs = [
        pl.BlockSpec((1, n_k, HEAD_DIM), lambda h, i: (h, 0, 0)),
        pl.BlockSpec((1, n_sub, VT_ROWS, ks), lambda h, i: (h, 0, 0, 0)),
    ]
    if diff:
        in_specs = [pl.BlockSpec((1, HEAD_DIM, tq), lambda h, i: (h, 0, i + qb0))] + kv_specs + [
            pl.BlockSpec((4, DIFF_QK_DIM), lambda h, i: (0, 0)),
            pl.BlockSpec((HEAD_DIM, 1), lambda h, i: (0, 0)),
        ]
        args = (qt, k, vt, lamp, gsub.reshape(HEAD_DIM, 1))
        out_spec = pl.BlockSpec((1, HEAD_DIM, tq), lambda h, i: (h, 0, i))
        out_shape = jax.ShapeDtypeStruct((nh, HEAD_DIM, n_q), BF16)
    else:
        in_specs = [pl.BlockSpec((1, groups, HEAD_DIM, tq), lambda h, i: (h, 0, 0, i + qb0))] + kv_specs
        args = (qt, k, vt)
        out_spec = pl.BlockSpec((1, groups, HEAD_DIM, tq), lambda h, i: (h, 0, 0, i))
        out_shape = jax.ShapeDtypeStruct((nh, groups, HEAD_DIM, n_q), BF16)
    return pl.pallas_call(
        kernel,
        grid=(nh, n_q // tq),
        in_specs=in_specs,
        out_specs=out_spec,
        out_shape=out_shape,
        scratch_shapes=([pltpu.VMEM((HEAD_DIM, tq), BF16)] * groups
                        + [pltpu.VMEM((1, tq), F32)] * groups
                        + [pltpu.VMEM((VT_ROWS, tq), F32)] * groups
                        + [pltpu.VMEM((1, tq), F32)] * (ATTN_SLOTS * groups)
                        + [pltpu.VMEM((tk, tq), F32)] * (ATTN_SLOTS * groups)),
        compiler_params=_cparams(("arbitrary", "arbitrary")),
    )(*args)


def _dft_stage1_kernel(f1_ref, u_ref, z_ref):
    z = jnp.dot(f1_ref[...], u_ref[...], preferred_element_type=F32)
    l1 = u_ref.shape[0]
    z_ref[0] = z[:l1].astype(z_ref.dtype)
    z_ref[1] = z[l1:].astype(z_ref.dtype)


def _dft_stage1_call(f1s, u2d):
    l1, n = u2d.shape
    tn = _pick(n, (2048, 1024, 512, 256, 128))
    return pl.pallas_call(
        _dft_stage1_kernel,
        grid=(n // tn,),
        in_specs=[pl.BlockSpec((2 * l1, l1), lambda j: (0, 0)),
                  pl.BlockSpec((l1, tn), lambda j: (0, j))],
        out_specs=pl.BlockSpec((2, l1, tn), lambda j: (0, 0, j)),
        out_shape=jax.ShapeDtypeStruct((2, l1, n), BF16),
        compiler_params=_cparams(("arbitrary",)),
    )(f1s, u2d)


def _dft_stage2_kernel(e_ref, z_ref, cs_ref, w_ref, o_ref, *, kb, scale, real_only):
    for b in range(kb):
        zr = z_ref[0, b]
        if real_only:
            e = e_ref[b]
            x = jnp.dot(e, zr, preferred_element_type=F32)
            l2 = zr.shape[0]
            xcat = jnp.concatenate([x[:l2], x[l2:]], axis=1)
        else:
            zi = z_ref[1, b]
            rhs = jnp.concatenate(
                [jnp.concatenate([zr, zi], axis=1), jnp.concatenate([zi, -zr], axis=1)], axis=0)
            xcat = jnp.dot(e_ref[b], rhs, preferred_element_type=F32)
        y1 = jnp.dot(xcat.astype(BF16), cs_ref[...], preferred_element_type=F32)
        y = jnp.dot(y1.astype(BF16), w_ref[...], preferred_element_type=F32)
        o_ref[b] = (y * scale).astype(o_ref.dtype)


def _dft_stage2_call(e2, z, cs, wbd, *, scale, real_only):
    nz, l1, l2, cw = z.shape
    kb = _pick(l1, (8, 4, 2, 1))
    kernel = functools.partial(_dft_stage2_kernel, kb=kb, scale=scale, real_only=real_only)
    return pl.pallas_call(
        kernel,
        grid=(l1 // kb,),
        in_specs=[pl.BlockSpec((kb,) + e2.shape[1:], lambda i: (i, 0, 0)),
                  pl.BlockSpec((nz, kb, l2, cw), lambda i: (0, i, 0, 0)),
                  pl.BlockSpec(cs.shape, lambda i: (0, 0)),
                  pl.BlockSpec(wbd.shape, lambda i: (0, 0))],
        out_specs=pl.BlockSpec((kb, l2, cw), lambda i: (i, 0, 0)),
        out_shape=jax.ShapeDtypeStruct((l1, l2, cw), BF16),
        compiler_params=_cparams(("arbitrary",)),
    )(e2, z, cs, wbd)


@functools.lru_cache(maxsize=None)
def _dft_tables(n):
    l2 = FFT_L2 if n > 2 * FFT_L2 and n % FFT_L2 == 0 else n
    l1 = n // l2
    k1 = np.arange(l1)
    ang1 = 2.0 * np.pi * ((k1[:, None] * k1[None, :]) % l1) / l1
    f1s = np.concatenate([np.cos(ang1), -np.sin(ang1)], axis=0)
    k = k1[:, None] + l1 * np.arange(l2)[None, :]
    n2 = np.arange(l2)
    ang = 2.0 * np.pi * ((k[:, :, None] * n2[None, None, :]) % n) / n
    if l1 == 1:
        e2 = np.concatenate([np.cos(ang), -np.sin(ang)], axis=1)
    else:
        e2 = np.concatenate([np.cos(ang), np.sin(ang)], axis=2)
    c = np.arange(FOURIER_GROUP_DIM)
    angc = 2.0 * np.pi * ((c[:, None] * c[None, :]) % FOURIER_GROUP_DIM) / FOURIER_GROUP_DIM
    eye = np.eye(N_FOURIER_GROUPS)
    cs = np.concatenate([np.kron(eye, np.cos(angc)), np.kron(eye, np.sin(angc))], axis=0)
    return l1, l2, f1s.astype(np.float32), e2.astype(np.float32), cs.astype(np.float32)


def _fnet_call(u, wbd):
    n, cw = u.shape
    l1, l2, f1s, e2, cs = _dft_tables(n)
    scale = 1.0 / math.sqrt(n * FOURIER_GROUP_DIM)
    e2 = jnp.asarray(e2, dtype=BF16)
    cs = jnp.asarray(cs, dtype=BF16)
    if l1 == 1:
        z = u.reshape(1, 1, l2, cw)
        y = _dft_stage2_call(e2, z, cs, wbd, scale=scale, real_only=True)
        return y.reshape(n, cw)
    z = _dft_stage1_call(jnp.asarray(f1s, dtype=BF16), u.reshape(l1, l2 * cw))
    y = _dft_stage2_call(e2, z.reshape(2, l1, l2, cw), cs, wbd, scale=scale, real_only=False)
    return jnp.transpose(y, (1, 0, 2)).reshape(n, cw)


def _store_token_tiles(ref, x):
    n, d = x.shape
    for s in range(d // LANES):
        ref[pl.ds(s, n, stride=SUBLANES), :] = x[:, s * LANES:(s + 1) * LANES]


def _load_token_tiles(ref, n):
    parts = [ref[pl.ds(s, n, stride=SUBLANES), :] for s in range(SUBLANES)]
    return jnp.concatenate(parts, axis=1)


def _out_kernel(mix_ref, x_ref, w_ref, m_ref, g2_ref, wr_ref, br_ref,
                xo_ref, h2_ref, gmat_ref, g4_ref, i4_ref):
    o = jnp.dot(mix_ref[...], w_ref[...], preferred_element_type=F32)
    xn = x_ref[...] + m_ref[0, 2:3, :] * o
    xo_ref[...] = xn
    ms = jnp.mean(xn * xn, axis=-1, keepdims=True)
    h2 = xn * lax.rsqrt(ms + NORM_EPS) * g2_ref[...]
    h2 = h2 * (1.0 + m_ref[0, 4:5, :]) + m_ref[0, 3:4, :]
    _store_token_tiles(h2_ref, h2)
    logits = jnp.dot(h2, wr_ref[...], precision=lax.Precision.HIGHEST,
                     preferred_element_type=F32) + br_ref[...]
    iota = lax.broadcasted_iota(jnp.int32, logits.shape, 1).astype(F32)
    vals, idxs = [], []
    l = logits
    for _ in range(TOP_K):
        mx = jnp.max(l, axis=1, keepdims=True)
        ik = jnp.min(jnp.where(l == mx, iota, float(ROUTER_PAD)), axis=1, keepdims=True)
        vals.append(mx)
        idxs.append(ik)
        l = jnp.where(iota == ik, -jnp.inf, l)
    es = [jnp.exp(v - vals[0]) for v in vals]
    den = es[0] + es[1] + es[2] + es[3]
    gs = [e / den for e in es]
    gmat = jnp.zeros(logits.shape, F32)
    for ik, g in zip(idxs, gs):
        gmat = jnp.where(iota == ik, g, gmat)
    gmat_ref[...] = gmat
    lane4 = lax.broadcasted_iota(jnp.int32, g4_ref.shape, 1)
    g4 = jnp.zeros(g4_ref.shape, F32)
    i4 = jnp.zeros(g4_ref.shape, F32)
    for k in range(TOP_K):
        g4 = jnp.where(lane4 == k, gs[k], g4)
        i4 = jnp.where(lane4 == k, idxs[k], i4)
    g4_ref[...] = g4
    i4_ref[...] = i4.astype(jnp.int32)


def _out_call(mix, xsrc, x_row0, w_out, mods, g2, wr, br, n_ctx_rows):
    n, d = mix.shape
    tm = _pick(n, (256, 128, 64, 32, 16), x_row0, n_ctx_rows)
    xb0 = x_row0 // tm
    nct = n_ctx_rows // tm
    row = lambda i: (i, 0)
    const = lambda i: (0, 0)
    return pl.pallas_call(
        _out_kernel,
        grid=(n // tm,),
        in_specs=[
            pl.BlockSpec((tm, d), row),
            pl.BlockSpec((tm, d), lambda i: (i + xb0, 0)),
            pl.BlockSpec((d, d), const),
            pl.BlockSpec((1, 6, d), lambda i: (jnp.where(i < nct, 1, 0), 0, 0)),
            pl.BlockSpec((1, d), const),
            pl.BlockSpec((d, ROUTER_PAD), const),
            pl.BlockSpec((1, ROUTER_PAD), const),
        ],
        out_specs=[
            pl.BlockSpec((tm, d), row),
            pl.BlockSpec((tm * SUBLANES, LANES), row),
            pl.BlockSpec((tm, ROUTER_PAD), row),
            pl.BlockSpec((tm, TOP_K), row),
            pl.BlockSpec((tm, TOP_K), row),
        ],
        out_shape=[
            jax.ShapeDtypeStruct((n, d), F32),
            jax.ShapeDtypeStruct((n * SUBLANES, LANES), F32),
            jax.ShapeDtypeStruct((n, ROUTER_PAD), F32),
            jax.ShapeDtypeStruct((n, TOP_K), F32),
            jax.ShapeDtypeStruct((n, TOP_K), jnp.int32),
        ],
        compiler_params=_cparams(("arbitrary",)),
    )(mix, xsrc, w_out, mods, g2.reshape(1, d), wr, br)


def _rank_kernel(g_ref, rank_ref, cnt_ref, carry_sc):
    @pl.when(pl.program_id(0) == 0)
    def _():
        carry_sc[...] = jnp.zeros(carry_sc.shape, F32)

    self32 = jnp.where(g_ref[...] > 0.0, 1.0, 0.0)
    sel = self32.astype(BF16)
    tm = sel.shape[0]
    r = lax.broadcasted_iota(jnp.int32, (tm, tm), 0)
    c = lax.broadcasted_iota(jnp.int32, (tm, tm), 1)
    tri = jnp.where(c < r, 1.0, 0.0).astype(BF16)
    excl = jnp.dot(tri, sel, preferred_element_type=F32) + carry_sc[0:1, :]
    rank_ref[...] = excl.astype(jnp.int32)
    tot = carry_sc[0:1, :] + jnp.sum(self32, axis=0, keepdims=True)
    carry_sc[...] = jnp.broadcast_to(tot, carry_sc.shape)
    cnt_ref[...] = jnp.broadcast_to(tot, cnt_ref.shape).astype(jnp.int32)


def _rank_call(gmat):
    n, w = gmat.shape
    tm = _pick(n, (256, 128, 64, 32, 16))
    return pl.pallas_call(
        _rank_kernel,
        grid=(n // tm,),
        in_specs=[pl.BlockSpec((tm, w), lambda i: (i, 0))],
        out_specs=[pl.BlockSpec((tm, w), lambda i: (i, 0)),
                   pl.BlockSpec((SUBLANES, w), lambda i: (0, 0))],
        out_shape=[jax.ShapeDtypeStruct((n, w), jnp.int32),
                   jax.ShapeDtypeStruct((SUBLANES, w), jnp.int32)],
        scratch_shapes=[pltpu.VMEM((SUBLANES, w), F32)],
        compiler_params=_cparams(("arbitrary",)),
    )(gmat)


def _dispatch_kernel(dest_ref, h_ref, xs_in_ref, xs_ref, sem, *, tt):
    del xs_in_ref
    base = pl.program_id(0) * (tt * TOP_K)

    def copy(t, k):
        src = pl.multiple_of(t * SUBLANES, SUBLANES)
        dst = pl.multiple_of(dest_ref[base + t * TOP_K + k] * SUBLANES, SUBLANES)
        return pltpu.make_async_copy(h_ref.at[pl.ds(src, SUBLANES)],
                                     xs_ref.at[pl.ds(dst, SUBLANES)], sem)

    def issue(t, c):
        for k in range(TOP_K):
            copy(t, k).start()
        return c

    def drain(t, c):
        for k in range(TOP_K):
            copy(t, k).wait()
        return c

    lax.fori_loop(0, tt, issue, 0)
    lax.fori_loop(0, tt, drain, 0)


def _dispatch_call(dest_flat, h2t, n_slots):
    n = h2t.shape[0] // SUBLANES
    tt = _pick(n, (256, 128, 64, 32, 16))
    xs0 = jnp.zeros((n_slots * SUBLANES, LANES), F32)
    return pl.pallas_call(
        functools.partial(_dispatch_kernel, tt=tt),
        grid_spec=pltpu.PrefetchScalarGridSpec(
            num_scalar_prefetch=1,
            grid=(n // tt,),
            in_specs=[pl.BlockSpec((tt * SUBLANES, LANES), lambda i, dref: (i, 0)),
                      pl.BlockSpec(memory_space=pl.ANY)],
            out_specs=pl.BlockSpec(memory_space=pl.ANY),
            scratch_shapes=[pltpu.SemaphoreType.DMA(())],
        ),
        out_shape=jax.ShapeDtypeStruct(xs0.shape, F32),
        input_output_aliases={2: 0},
        compiler_params=_cparams(("arbitrary",)),
    )(dest_flat, h2t, xs0)


def _expert_kernel(be_ref, nb_ref, x_ref, wgu_ref, bgu_ref, wd_ref, bd_ref, y_ref, wgu_sc, wd_sc):
    b = pl.program_id(0)
    active = b < nb_ref[0]

    @pl.when(jnp.logical_and(active, jnp.logical_or(b == 0, be_ref[b] != be_ref[jnp.maximum(b - 1, 0)])))
    def _():
        wgu_sc[...] = wgu_ref[0].astype(BF16)
        wd_sc[...] = wd_ref[0].astype(BF16)

    @pl.when(active)
    def _():
        x = _load_token_tiles(x_ref, MOE_BLOCK).astype(BF16)
        gu = jnp.dot(x, wgu_sc[...], preferred_element_type=F32) + bgu_ref[0]
        glu = jnp.minimum(gu[:, :EXPERT_FF], SWIGLU_LIMIT)
        lin = jnp.clip(gu[:, EXPERT_FF:], -SWIGLU_LIMIT, SWIGLU_LIMIT)
        act = glu * jax.nn.sigmoid(SWIGLU_ALPHA * glu) * (lin + 1.0)
        y = jnp.dot(act.astype(BF16), wd_sc[...], preferred_element_type=F32) + bd_ref[0]
        _store_token_tiles(y_ref, y)

    @pl.when(pl.program_id(0) >= nb_ref[0])
    def _():
        y_ref[...] = jnp.zeros(y_ref.shape, F32)


def _expert_call(blk_e, nb_used, xs, wgu, bgu, wd, bd):
    nb = xs.shape[0] // (MOE_BLOCK * SUBLANES)
    ne, d, ff2 = wgu.shape
    blk = lambda b, be, nbu: (jnp.minimum(b, nbu[0] - 1), 0)
    wsel = lambda b, be, nbu: (be[jnp.minimum(b, nbu[0] - 1)], 0, 0)
    return pl.pallas_call(
        _expert_kernel,
        grid_spec=pltpu.PrefetchScalarGridSpec(
            num_scalar_prefetch=2,
            grid=(nb,),
            in_specs=[
                pl.BlockSpec((MOE_BLOCK * SUBLANES, LANES), blk),
                pl.BlockSpec((1, d, ff2), wsel),
                pl.BlockSpec((1, 1, ff2), wsel),
                pl.BlockSpec((1, ff2 // 2, d), wsel),
                pl.BlockSpec((1, 1, d), wsel),
            ],
            out_specs=pl.BlockSpec((MOE_BLOCK * SUBLANES, LANES), lambda b, be, nbu: (b, 0)),
            scratch_shapes=[pltpu.VMEM((d, ff2), BF16), pltpu.VMEM((ff2 // 2, d), BF16)],
        ),
        out_shape=jax.ShapeDtypeStruct(xs.shape, F32),
        compiler_params=_cparams(("arbitrary",)),
    )(blk_e, nb_used, xs, wgu, bgu.reshape(ne, 1, ff2), wd, bd.reshape(ne, 1, d))


def _combine_kernel(dest_ref, ys_ref, x_ref, g4_ref, m_ref, gf_ref, o_ref, buf, sem, *, tm, final):
    base = pl.program_id(0) * tm * TOP_K

    def copy(t, k):
        src = pl.multiple_of(dest_ref[base + t * TOP_K + k] * SUBLANES, SUBLANES)
        dst = pl.multiple_of(t * SUBLANES, SUBLANES)
        return pltpu.make_async_copy(ys_ref.at[pl.ds(src, SUBLANES)],
                                     buf.at[k, pl.ds(dst, SUBLANES)], sem)

    def issue(t, c):
        for k in range(TOP_K):
            copy(t, k).start()
        return c

    def drain(t, c):
        for k in range(TOP_K):
            copy(t, k).wait()
        return c

    lax.fori_loop(0, tm, issue, 0)
    lax.fori_loop(0, tm, drain, 0)

    g4 = g4_ref[...]
    y = _load_token_tiles(buf.at[0], tm) * g4[:, 0:1]
    for k in range(1, TOP_K):
        y = y + _load_token_tiles(buf.at[k], tm) * g4[:, k:k + 1]
    xo = x_ref[...] + m_ref[0, 5:6, :] * y
    if final:
        ms = jnp.mean(xo * xo, axis=-1, keepdims=True)
        xo = xo * lax.rsqrt(ms + NORM_EPS) * gf_ref[...]
    o_ref[...] = xo


def _combine_call(dest_flat, ys, x, g4, mods, g_final, n_ctx_rows, final):
    n, d = x.shape
    tm = _pick(n, (256, 128, 64, 32, 16), n_ctx_rows)
    nct = n_ctx_rows // tm
    row = lambda i, dref: (i, 0)
    return pl.pallas_call(
        functools.partial(_combine_kernel, tm=tm, final=final),
        grid_spec=pltpu.PrefetchScalarGridSpec(
            num_scalar_prefetch=1,
            grid=(n // tm,),
            in_specs=[
                pl.BlockSpec(memory_space=pl.ANY),
                pl.BlockSpec((tm, d), row),
                pl.BlockSpec((tm, TOP_K), row),
                pl.BlockSpec((1, 6, d), lambda i, dref: (jnp.where(i < nct, 1, 0), 0, 0)),
                pl.BlockSpec((1, d), lambda i, dref: (0, 0)),
            ],
            out_specs=pl.BlockSpec((tm, d), row),
            scratch_shapes=[pltpu.VMEM((TOP_K, tm * SUBLANES, LANES), F32),
                            pltpu.SemaphoreType.DMA(())],
        ),
        out_shape=jax.ShapeDtypeStruct((n, d), F32),
        compiler_params=_cparams(("arbitrary",)),
    )(dest_flat, ys, x, g4, mods, g_final.reshape(1, d))


def _moe(x_mid, h2t, gmat, g4, i4, mods, n_ctx_rows, wgu, bgu, wd, bd, g_final, final):
    n, d = x_mid.shape
    assert d == SUBLANES * LANES
    rank, cnt = _rank_call(gmat)
    counts = cnt[0, :N_EXPERTS]
    padded = (counts + MOE_BLOCK - 1) // MOE_BLOCK * MOE_BLOCK
    ends = jnp.cumsum(padded)
    starts = ends - padded
    dest = starts[i4] + jnp.take_along_axis(rank, i4, axis=1)
    dest_flat = dest.reshape(-1).astype(jnp.int32)
    n_slots = (n * TOP_K + MOE_BLOCK - 1) // MOE_BLOCK * MOE_BLOCK + N_EXPERTS * MOE_BLOCK
    nb = n_slots // MOE_BLOCK
    blk_start = jnp.arange(nb, dtype=jnp.int32) * MOE_BLOCK
    blk_e = jnp.minimum(jnp.sum((ends[None, :] <= blk_start[:, None]).astype(jnp.int32), axis=1),
                        N_EXPERTS - 1).astype(jnp.int32)
    nb_used = (ends[-1] // MOE_BLOCK).astype(jnp.int32).reshape(1)
    xs = _dispatch_call(dest_flat, h2t, n_slots)
    ys = _expert_call(blk_e, nb_used, xs, wgu, bgu, wd, bd)
    return _combine_call(dest_flat, ys, x_mid, g4, mods, g_final, n_ctx_rows, final)


def _heads(a, nh):
    r = a.shape[0]
    return jnp.transpose(a.reshape(r, nh, HEAD_DIM), (1, 0, 2))


def _unheads(o):
    n = o.shape[-1]
    o = o.reshape(-1, HEAD_DIM, n)
    return jnp.transpose(o, (2, 0, 1)).reshape(n, -1)


def kernel(x, c, ctx, c_ctx, w_mod, b_mod, g_norm1, g_norm2, w_in, w_out, w_fnet, lambda_q1, lambda_k1, lambda_q2, lambda_k2, g_subln, g_qnorm, g_knorm, w_router, b_router, w_gate_up, b_gate_up, w_down, b_down, g_final):
    bsz, seq, d = x.shape
    assert bsz == 1
    n_ctx = ctx.shape[1]
    depth = w_mod.shape[0]
    r = n_ctx + seq

    cc = jnp.zeros((SUBLANES, d), F32).at[0].set(c[0]).at[1].set(c_ctx)
    mod_all = _mod_call(cc, w_mod, b_mod)
    tabs = _rope_tables(n_ctx, seq)
    bdq = _block_diag_mean(GQA_WIDTH, HEAD_DIM)
    bdk = _block_diag_mean(GQA_KV_WIDTH, HEAD_DIM)

    xall = jnp.concatenate([ctx[0], x[0]], axis=0)
    out = None
    for l in range(depth):
        need_ctx = l < depth - 1
        lam_init = 0.8 - 0.6 * math.exp(-0.3 * l)
        mods = mod_all[l, :2].reshape(2, 6, d)
        wl = w_in[l]
        w_ext = jnp.concatenate([
            wl,
            _swap_halves_cols(wl[:, _C_DQ:_C_DK], DIFF_QK_DIM),
            _swap_halves_cols(wl[:, _C_DK:_C_DV], DIFF_QK_DIM),
            _swap_halves_cols(wl[:, _C_GQ:_C_GK], HEAD_DIM),
            _swap_halves_cols(wl[:, _C_GK:_C_GV], HEAD_DIM),
        ], axis=1).astype(BF16)
        gq = g_qnorm[l]
        gk = g_knorm[l]
        gq_sw = jnp.concatenate([gq[HEAD_DIM // 2:], gq[:HEAD_DIM // 2]])
        gk_sw = jnp.concatenate([gk[HEAD_DIM // 2:], gk[:HEAD_DIM // 2]])
        gq2 = jnp.stack([jnp.tile(gq, N_GQA_Q_HEADS), jnp.tile(gq_sw, N_GQA_Q_HEADS)])
        gk2 = jnp.stack([jnp.tile(gk, N_GQA_KV_HEADS), jnp.tile(gk_sw, N_GQA_KV_HEADS)])

        f, dq, dk, dv, gqo, gko, gvo = _in_call(xall, g_norm1[l], mods, w_ext, tabs, gq2, gk2,
                                                 bdq, bdk, n_ctx)

        dqh = jnp.transpose(dq).reshape(N_DIFF_HEADS, HEAD_DIM, r)
        dkh = _heads(dk, N_DIFF_HEADS)
        dvh = _heads(dv, N_DIFF_HEADS)
        gqh = jnp.transpose(gqo).reshape(N_GQA_KV_HEADS, GQA_GROUP, HEAD_DIM, r)
        gkh = _heads(gko, N_GQA_KV_HEADS)
        gvh = _heads(gvo, N_GQA_KV_HEADS)
        lamp = jnp.stack([lambda_q1[l], lambda_k1[l], lambda_q2[l], lambda_k2[l]]).astype(F32)
        gsub = g_subln[l].reshape(1, HEAD_DIM)
        wbd = jnp.zeros((FOURIER_WIDTH, FOURIER_WIDTH), F32)
        for g in range(N_FOURIER_GROUPS):
            sl = slice(g * FOURIER_GROUP_DIM, (g + 1) * FOURIER_GROUP_DIM)
            wbd = wbd.at[sl, sl].set(w_fnet[l, g])
        wbd = wbd.astype(BF16)

        def mixers(q_row0, n_q, n_k):
            d_o = _attn_call(dqh, dkh, dvh, diff=True, q_row0=q_row0, n_q=n_q, n_k=n_k,
                             lamp=lamp, gsub=gsub, lam_scale=1.0 - lam_init)
            g_o = _attn_call(gqh, gkh, gvh, diff=False, q_row0=q_row0, n_q=n_q, n_k=n_k)
            f_o = _fnet_call(f[q_row0:q_row0 + n_q], wbd)
            return jnp.concatenate([f_o, _unheads(d_o), _unheads(g_o)], axis=1)

        mix = mixers(n_ctx, seq, r)
        if need_ctx:
            mix = jnp.concatenate([mixers(0, n_ctx, n_ctx), mix], axis=0)
            x_row0, nct = 0, n_ctx
        else:
            x_row0, nct = n_ctx, 0

        wr = jnp.zeros((d, ROUTER_PAD), F32).at[:, :N_EXPERTS].set(w_router[l])
        br = jnp.full((1, ROUTER_PAD), NEG_BIG, F32).at[0, :N_EXPERTS].set(b_router[l])
        x_mid, h2, gmat, g4, i4 = _out_call(mix, xall, x_row0, w_out[l].astype(BF16), mods,
                                            g_norm2[l], wr, br, nct)
        xall = _moe(x_mid, h2, gmat, g4, i4, mods, nct,
                    w_gate_up[l], b_gate_up[l], w_down[l], b_down[l],
                    g_final, final=not need_ctx)
    return xall.reshape(1, seq, d)
```

```python
import functools
import math

import numpy as np
import jax
import jax.numpy as jnp
from jax import lax
from jax.experimental import pallas as pl
from jax.experimental.pallas import tpu as pltpu

F32 = jnp.float32
BF16 = jnp.bfloat16

GRID_W = 64
HEAD_DIM = 64
N_FOURIER_GROUPS = 4
FOURIER_GROUP_DIM = 64
FOURIER_WIDTH = N_FOURIER_GROUPS * FOURIER_GROUP_DIM
N_DIFF_HEADS = 6
DIFF_QK_DIM = HEAD_DIM // 2
DIFF_WIDTH = N_DIFF_HEADS * HEAD_DIM
N_GQA_Q_HEADS = 6
N_GQA_KV_HEADS = 2
GQA_GROUP = N_GQA_Q_HEADS // N_GQA_KV_HEADS
GQA_WIDTH = N_GQA_Q_HEADS * HEAD_DIM
GQA_KV_WIDTH = N_GQA_KV_HEADS * HEAD_DIM
N_EXPERTS = 32
TOP_K = 4
EXPERT_FF = 1024
SWIGLU_ALPHA = 1.702
SWIGLU_LIMIT = 7.0
ROPE_THETA = 10000.0
NORM_EPS = 1e-6
SUBLN_EPS = 1e-5
LOG2E = math.log2(math.e)
DIFF_SCALE = DIFF_QK_DIM ** -0.5 * LOG2E
GQA_SCALE = HEAD_DIM ** -0.5 * LOG2E

LANES = 128
SUBLANES = 8
VMEM_LIMIT = 56 * 1024 * 1024

ROUTER_PAD = LANES
NEG_BIG = -1e30
MOE_BLOCK = 256
ATTN_TK = 1280
ATTN_SLOTS = 3
ATTN_QT = 4
FFT_L2 = 128


def _pick(n, prefs, *offsets):
    g = n
    for o in offsets:
        g = math.gcd(g, o)
    for p in prefs:
        if g % p == 0:
            return p
    return g


def _cparams(sem):
    return pltpu.CompilerParams(dimension_semantics=sem, vmem_limit_bytes=VMEM_LIMIT)


def _mod_kernel(c_ref, w_ref, b_ref, o_ref):
    c = c_ref[...]
    s = c * jax.nn.sigmoid(c)
    o_ref[0] = jnp.dot(s, w_ref[0], precision=lax.Precision.HIGHEST,
                       preferred_element_type=F32) + b_ref[0]


def _mod_call(cc, w_mod, b_mod):
    depth, d, n6 = w_mod.shape
    tn = _pick(n6, (1536, 1024, 512, 256, 128))
    return pl.pallas_call(
        _mod_kernel,
        grid=(depth, n6 // tn),
        in_specs=[
            pl.BlockSpec((SUBLANES, d), lambda l, j: (0, 0)),
            pl.BlockSpec((1, d, tn), lambda l, j: (l, 0, j)),
            pl.BlockSpec((1, 1, tn), lambda l, j: (l, 0, j)),
        ],
        out_specs=pl.BlockSpec((1, SUBLANES, tn), lambda l, j: (l, 0, j)),
        out_shape=jax.ShapeDtypeStruct((depth, SUBLANES, n6), F32),
        compiler_params=_cparams(("arbitrary", "arbitrary")),
    )(cc, w_mod, b_mod.reshape(depth, 1, n6))


_C_F = 0
_C_DQ = _C_F + FOURIER_WIDTH
_C_DK = _C_DQ + DIFF_WIDTH
_C_DV = _C_DK + DIFF_WIDTH
_C_GQ = _C_DV + DIFF_WIDTH
_C_GK = _C_GQ + GQA_WIDTH
_C_GV = _C_GK + GQA_KV_WIDTH
_C_DQS = _C_GV + GQA_KV_WIDTH
_C_DKS = _C_DQS + DIFF_WIDTH
_C_GQS = _C_DKS + DIFF_WIDTH
_C_GKS = _C_GQS + GQA_WIDTH
_C_END = _C_GKS + GQA_KV_WIDTH


def _swap_halves_cols(w, chunk):
    k, n = w.shape
    w = w.reshape(k, n // chunk, 2, chunk // 2)
    return w[:, :, ::-1, :].reshape(k, n)


def _head_mean_sq(z, bd):
    q2 = z * z
    hi = q2.astype(BF16)
    lo = (q2 - hi.astype(F32)).astype(BF16)
    return (jnp.dot(hi, bd, preferred_element_type=F32)
            + jnp.dot(lo, bd, preferred_element_type=F32))


def _in_kernel(x_ref, g_ref, m_ref, w_ref, cd_ref, sd_ref, cg_ref, sg_ref,
               gq_ref, gk_ref, bdq_ref, bdk_ref,
               f_ref, dq_ref, dk_ref, dv_ref, gqo_ref, gko_ref, gvo_ref):
    x = x_ref[...]
    ms = jnp.mean(x * x, axis=-1, keepdims=True)
    y = x * lax.rsqrt(ms + NORM_EPS) * g_ref[...]
    h = y * (1.0 + m_ref[0, 1:2, :]) + m_ref[0, 0:1, :]
    z = jnp.dot(h.astype(BF16), w_ref[...], preferred_element_type=F32)

    f_ref[...] = z[:, _C_F:_C_DQ].astype(BF16)
    dv_ref[...] = z[:, _C_DV:_C_GQ].astype(BF16)
    gvo_ref[...] = z[:, _C_GV:_C_DQS].astype(BF16)

    cd = cd_ref[...]
    sd = sd_ref[...]
    cd3 = jnp.concatenate([cd, cd, cd], axis=1)
    sd3 = jnp.concatenate([sd, sd, sd], axis=1)
    dq = z[:, _C_DQ:_C_DK] * cd3 + z[:, _C_DQS:_C_DKS] * sd3
    dq_ref[...] = (dq * DIFF_SCALE).astype(BF16)
    dk = z[:, _C_DK:_C_DV] * cd3 + z[:, _C_DKS:_C_GQS] * sd3
    dk_ref[...] = dk.astype(BF16)

    cg = cg_ref[...]
    sg = sg_ref[...]
    cg3 = jnp.concatenate([cg, cg, cg], axis=1)
    sg3 = jnp.concatenate([sg, sg, sg], axis=1)
    zq = z[:, _C_GQ:_C_GK]
    rq = lax.rsqrt(_head_mean_sq(zq, bdq_ref[...]) + NORM_EPS)
    gq = rq * (zq * (gq_ref[0:1, :] * cg3) + z[:, _C_GQS:_C_GKS] * (gq_ref[1:2, :] * sg3))
    gqo_ref[...] = (gq * GQA_SCALE).astype(BF16)
    zk = z[:, _C_GK:_C_GV]
    rk = lax.rsqrt(_head_mean_sq(zk, bdk_ref[...]) + NORM_EPS)
    gk = rk * (zk * (gk_ref[0:1, :] * cg) + z[:, _C_GKS:_C_END] * (gk_ref[1:2, :] * sg))
    gko_ref[...] = gk.astype(BF16)


def _in_call(xall, g1, mods, w_ext, tabs, gq2, gk2, bdq, bdk, n_ctx_rows):
    r, d = xall.shape
    tm = _pick(r, (256, 128, 64, 32, 16), n_ctx_rows)
    nct = n_ctx_rows // tm
    cd, sd, cg, sg = tabs
    row = lambda i: (i, 0)
    const = lambda i: (0, 0)
    widths = (FOURIER_WIDTH, DIFF_WIDTH, DIFF_WIDTH, DIFF_WIDTH, GQA_WIDTH, GQA_KV_WIDTH, GQA_KV_WIDTH)
    return pl.pallas_call(
        _in_kernel,
        grid=(r // tm,),
        in_specs=[
            pl.BlockSpec((tm, d), row),
            pl.BlockSpec((1, d), const),
            pl.BlockSpec((1, 6, d), lambda i: (jnp.where(i < nct, 1, 0), 0, 0)),
            pl.BlockSpec((d, _C_END), const),
            pl.BlockSpec((tm, LANES), row),
            pl.BlockSpec((tm, LANES), row),
            pl.BlockSpec((tm, LANES), row),
            pl.BlockSpec((tm, LANES), row),
            pl.BlockSpec((2, GQA_WIDTH), const),
            pl.BlockSpec((2, GQA_KV_WIDTH), const),
            pl.BlockSpec((GQA_WIDTH, GQA_WIDTH), const),
            pl.BlockSpec((GQA_KV_WIDTH, GQA_KV_WIDTH), const),
        ],
        out_specs=[pl.BlockSpec((tm, w), row) for w in widths],
        out_shape=[jax.ShapeDtypeStruct((r, w), BF16) for w in widths],
        compiler_params=_cparams(("arbitrary",)),
    )(xall, g1.reshape(1, d), mods, w_ext, cd, sd, cg, sg, gq2, gk2, bdq, bdk)


def _rope_tables(n_ctx, seq):
    rows = seq // GRID_W
    row_ids = jnp.repeat(jnp.arange(rows, dtype=F32), GRID_W)
    col_ids = jnp.tile(jnp.arange(GRID_W, dtype=F32), rows)

    def tab(rot_dim):
        n = rot_dim // 4
        inv = ROPE_THETA ** (-jnp.arange(n, dtype=F32) / n)
        ang = jnp.concatenate([row_ids[:, None] * inv, col_ids[:, None] * inv], axis=-1)
        cos, sin = jnp.cos(ang), jnp.sin(ang)
        reps = LANES // rot_dim
        c = jnp.tile(jnp.concatenate([cos, cos], axis=-1), (1, reps))
        s = jnp.tile(jnp.concatenate([-sin, sin], axis=-1), (1, reps))
        c = jnp.concatenate([jnp.ones((n_ctx, LANES), F32), c], axis=0)
        s = jnp.concatenate([jnp.zeros((n_ctx, LANES), F32), s], axis=0)
        return c, s

    cd, sd = tab(DIFF_QK_DIM)
    cg, sg = tab(HEAD_DIM)
    return cd, sd, cg, sg


def _block_diag_mean(width, chunk):
    a = np.kron(np.eye(width // chunk), np.full((chunk, chunk), 1.0 / chunk))
    return jnp.asarray(a, dtype=BF16)


VT_ROWS = 80


def _attn_kernel(*refs, diff, tq, n_qt, ks, spc, nk, lam_scale):
    groups = 2 if diff else GQA_GROUP
    n_in = 5 if diff else 3
    tk = spc * ks
    o_ref = refs[n_in]
    sc = list(refs[n_in + 1:])

    def take(n):
        out = sc[:n]
        del sc[:n]
        return out

    qt_sc, m_sc, acc_sc = take(groups), take(groups), take(groups)
    slots, skew = ATTN_SLOTS, ATTN_SLOTS - 1
    cmax_sc = [take(groups) for _ in range(slots)]
    s_sc = [take(groups) for _ in range(slots)]
    if diff:
        qt_ref, k_ref, vt_ref, lamp_ref, gsub_ref = refs[:n_in]
    else:
        qt_ref, k_ref, vt_ref = refs[:n_in]
    for t in range(n_qt):
        cols = pl.ds(t * tq, tq)
        if diff:
            qt = qt_ref[0, :, cols]
            row = lax.broadcasted_iota(jnp.int32, qt.shape, 0)
            zero = jnp.zeros_like(qt)
            qt_sc[0][t] = jnp.where(row < DIFF_QK_DIM, qt, zero)
            qt_sc[1][t] = jnp.where(row >= DIFF_QK_DIM, qt, zero)
        else:
            for g in range(groups):
                qt_sc[g][t] = qt_ref[0, g, :, cols]
    for g in range(groups):
        m_sc[g][...] = jnp.full(m_sc[g].shape, -jnp.inf, F32)
        acc_sc[g][...] = jnp.zeros(acc_sc[g].shape, F32)

    def key_rows(c, a):
        start = c * tk + a * ks
        if not isinstance(start, int):
            start = pl.multiple_of(start, ks)
        return k_ref[0, pl.ds(start, ks), :]

    def step(r, val, sco):
        vs, ss = r % slots, (r + skew) % slots
        rows = [pl.ds(a * ks, ks) for a in range(spc)]
        if val is not None:
            tv, cv = val
            m_new, alpha = [], []
            for g in range(groups):
                m_old = m_sc[g][tv]
                m_new.append(jnp.maximum(m_old, cmax_sc[vs][g][...]))
                alpha.append(jnp.exp2(m_old - m_new[g]))
                m_sc[g][tv] = m_new[g]
        if sco is not None:
            ts, cs = sco
        pv = [None] * groups
        cm = [None] * groups
        for a in range(spc):
            if sco is not None:
                ka = key_rows(cs, a)
                for g in range(groups):
                    st = jnp.dot(ka, qt_sc[g][ts], preferred_element_type=F32)
                    s_sc[ss][g][rows[a], :] = st
                    ca = jnp.max(st, axis=0, keepdims=True)
                    cm[g] = ca if cm[g] is None else jnp.maximum(cm[g], ca)
            if val is not None:
                for g in range(groups):
                    pt = jnp.exp2(s_sc[vs][g][rows[a], :] - m_new[g]).astype(BF16)
                    d = jnp.dot(vt_ref[0, cv * spc + a], pt, preferred_element_type=F32)
                    pv[g] = d if pv[g] is None else pv[g] + d
        for g in range(groups):
            if sco is not None:
                cmax_sc[ss][g][...] = cm[g]
            if val is not None:
                acc_sc[g][tv] = acc_sc[g][tv] * alpha[g] + pv[g]

    total = n_qt * nk
    flat = lambda f: divmod(f, nk)
    for f in range(min(skew, total)):
        step(f - skew, None, flat(f))
    n_both = max(total - skew, 0)

    def advance(t, c):
        last = c == nk - 1
        return jnp.where(last, t + 1, t), jnp.where(last, 0, c + 1)

    def ring(i, carry):
        tv, cv, ts, cs = carry
        for r in range(slots):
            step(r, (tv, cv), (ts, cs))
            tv, cv = advance(tv, cv)
            ts, cs = advance(ts, cs)
        return tv, cv, ts, cs

    if n_both // slots:
        init = tuple(jnp.int32(x) for x in flat(0) + flat(skew))
        lax.fori_loop(0, n_both // slots, ring, init)
    for f in range(n_both - n_both % slots, total):
        step(f, flat(f), flat(f + skew) if f < n_both else None)

    for t in range(n_qt):
        cols = pl.ds(t * tq, tq)
        outs = []
        for g in range(groups):
            acc = acc_sc[g][t]
            outs.append(acc[:HEAD_DIM] / acc[HEAD_DIM:HEAD_DIM + 1])
        if diff:
            lp = lamp_ref[...]
            lam = (jnp.exp(jnp.sum(lp[0:1] * lp[1:2], axis=1, keepdims=True))
                   - jnp.exp(jnp.sum(lp[2:3] * lp[3:4], axis=1, keepdims=True))
                   + (1.0 - lam_scale))
            od = outs[0] - lam * outs[1]
            ms = jnp.mean(od * od, axis=0, keepdims=True)
            o_ref[0, :, cols] = (od * lax.rsqrt(ms + SUBLN_EPS) * gsub_ref[...]
                                 * lam_scale).astype(o_ref.dtype)
        else:
            for g in range(groups):
                o_ref[0, g, :, cols] = outs[g].astype(o_ref.dtype)


def _attn_call(qt, k, v, *, diff, q_row0, n_q, n_k, lamp=None, gsub=None, lam_scale=1.0):
    nh = qt.shape[0]
    tq = _pick(n_q, (256, 128))
    n_qt = _pick(n_q // tq, (ATTN_QT, 2, 1))
    bq = n_qt * tq
    ks = _pick(n_k, (256, 128))
    n_sub = n_k // ks
    spc = _pick(n_sub, tuple(range(ATTN_TK // ks, 0, -1)))
    nk = n_sub // spc
    tk = spc * ks
    groups = 2 if diff else GQA_GROUP
    kernel = functools.partial(_attn_kernel, diff=diff, tq=tq, n_qt=n_qt, ks=ks, spc=spc, nk=nk,
                               lam_scale=lam_scale)
    qt = qt[..., q_row0:q_row0 + n_q]
    vk = v[:, :n_k]
    vt = jnp.concatenate([vk, jnp.ones(vk.shape[:2] + (1,), vk.dtype),
                          jnp.zeros(vk.shape[:2] + (VT_ROWS - HEAD_DIM - 1,), vk.dtype)], axis=-1)
    vt = jnp.transpose(vt.reshape(vk.shape[0], n_sub, ks, VT_ROWS), (0, 1, 3, 2))
    kv_specs = [
        pl.BlockSpec((1, n_k, HEAD_DIM), lambda h, i: (h, 0, 0)),
        pl.BlockSpec((1, n_sub, VT_ROWS, ks), lambda h, i: (h, 0, 0, 0)),
    ]
    if diff:
        in_specs = [pl.BlockSpec((1, HEAD_DIM, bq), lambda h, i: (h, 0, i))] + kv_specs + [
            pl.BlockSpec((4, DIFF_QK_DIM), lambda h, i: (0, 0)),
            pl.BlockSpec((HEAD_DIM, 1), lambda h, i: (0, 0)),
        ]
        args = (qt, k, vt, lamp, gsub.reshape(HEAD_DIM, 1))
        out_spec = pl.BlockSpec((1, HEAD_DIM, bq), lambda h, i: (h, 0, i))
        out_shape = jax.ShapeDtypeStruct((nh, HEAD_DIM, n_q), BF16)
    else:
        in_specs = [pl.BlockSpec((1, groups, HEAD_DIM, bq), lambda h, i: (h, 0, 0, i))] + kv_specs
        args = (qt, k, vt)
        out_spec = pl.BlockSpec((1, groups, HEAD_DIM, bq), lambda h, i: (h, 0, 0, i))
        out_shape = jax.ShapeDtypeStruct((nh, groups, HEAD_DIM, n_q), BF16)
    return pl.pallas_call(
        kernel,
        grid=(nh, n_q // bq),
        in_specs=in_specs,
        out_specs=out_spec,
        out_shape=out_shape,
        scratch_shapes=([pltpu.VMEM((n_qt, HEAD_DIM, tq), BF16)] * groups
                        + [pltpu.VMEM((n_qt, 1, tq), F32)] * groups
                        + [pltpu.VMEM((n_qt, VT_ROWS, tq), F32)] * groups
                        + [pltpu.VMEM((1, tq), F32)] * (ATTN_SLOTS * groups)
                        + [pltpu.VMEM((tk, tq), F32)] * (ATTN_SLOTS * groups)),
        compiler_params=_cparams(("arbitrary", "arbitrary")),
    )(*args)


def _dft_stage1_kernel(f1_ref, u_ref, z_ref):
    z = jnp.dot(f1_ref[...], u_ref[...], preferred_element_type=F32)
    l1 = u_ref.shape[0]
    z_ref[0] = z[:l1].astype(z_ref.dtype)
    z_ref[1] = z[l1:].astype(z_ref.dtype)


def _dft_stage1_call(f1s, u2d):
    l1, n = u2d.shape
    tn = _pick(n, (2048, 1024, 512, 256, 128))
    return pl.pallas_call(
        _dft_stage1_kernel,
        grid=(n // tn,),
        in_specs=[pl.BlockSpec((2 * l1, l1), lambda j: (0, 0)),
                  pl.BlockSpec((l1, tn), lambda j: (0, j))],
        out_specs=pl.BlockSpec((2, l1, tn), lambda j: (0, 0, j)),
        out_shape=jax.ShapeDtypeStruct((2, l1, n), BF16),
        compiler_params=_cparams(("arbitrary",)),
    )(f1s, u2d)


def _dft_stage2_kernel(e_ref, z_ref, cs_ref, w_ref, o_ref, *, kb, scale, real_only):
    for b in range(kb):
        zr = z_ref[0, b]
        if real_only:
            e = e_ref[b]
            x = jnp.dot(e, zr, preferred_element_type=F32)
            l2 = zr.shape[0]
            xcat = jnp.concatenate([x[:l2], x[l2:]], axis=1)
        else:
            zi = z_ref[1, b]
            rhs = jnp.concatenate(
                [jnp.concatenate([zr, zi], axis=1), jnp.concatenate([zi, -zr], axis=1)], axis=0)
            xcat = jnp.dot(e_ref[b], rhs, preferred_element_type=F32)
        y1 = jnp.dot(xcat.astype(BF16), cs_ref[...], preferred_element_type=F32)
        y = jnp.dot(y1.astype(BF16), w_ref[...], preferred_element_type=F32)
        o_ref[b] = (y * scale).astype(o_ref.dtype)


def _dft_stage2_call(e2, z, cs, wbd, *, scale, real_only):
    nz, l1, l2, cw = z.shape
    kb = _pick(l1, (8, 4, 2, 1))
    kernel = functools.partial(_dft_stage2_kernel, kb=kb, scale=scale, real_only=real_only)
    return pl.pallas_call(
        kernel,
        grid=(l1 // kb,),
        in_specs=[pl.BlockSpec((kb,) + e2.shape[1:], lambda i: (i, 0, 0)),
                  pl.BlockSpec((nz, kb, l2, cw), lambda i: (0, i, 0, 0)),
                  pl.BlockSpec(cs.shape, lambda i: (0, 0)),
                  pl.BlockSpec(wbd.shape, lambda i: (0, 0))],
        out_specs=pl.BlockSpec((kb, l2, cw), lambda i: (i, 0, 0)),
        out_shape=jax.ShapeDtypeStruct((l1, l2, cw), BF16),
        compiler_params=_cparams(("arbitrary",)),
    )(e2, z, cs, wbd)


@functools.lru_cache(maxsize=None)
def _dft_tables(n):
    l2 = FFT_L2 if n > 2 * FFT_L2 and n % FFT_L2 == 0 else n
    l1 = n // l2
    k1 = np.arange(l1)
    ang1 = 2.0 * np.pi * ((k1[:, None] * k1[None, :]) % l1) / l1
    f1s = np.concatenate([np.cos(ang1), -np.sin(ang1)], axis=0)
    k = k1[:, None] + l1 * np.arange(l2)[None, :]
    n2 = np.arange(l2)
    ang = 2.0 * np.pi * ((k[:, :, None] * n2[None, None, :]) % n) / n
    if l1 == 1:
        e2 = np.concatenate([np.cos(ang), -np.sin(ang)], axis=1)
    else:
        e2 = np.concatenate([np.cos(ang), np.sin(ang)], axis=2)
    c = np.arange(FOURIER_GROUP_DIM)
    angc = 2.0 * np.pi * ((c[:, None] * c[None, :]) % FOURIER_GROUP_DIM) / FOURIER_GROUP_DIM
    eye = np.eye(N_FOURIER_GROUPS)
    cs = np.concatenate([np.kron(eye, np.cos(angc)), np.kron(eye, np.sin(angc))], axis=0)
    return l1, l2, f1s.astype(np.float32), e2.astype(np.float32), cs.astype(np.float32)


def _fnet_call(u, wbd):
    n, cw = u.shape
    l1, l2, f1s, e2, cs = _dft_tables(n)
    scale = 1.0 / math.sqrt(n * FOURIER_GROUP_DIM)
    e2 = jnp.asarray(e2, dtype=BF16)
    cs = jnp.asarray(cs, dtype=BF16)
    if l1 == 1:
        z = u.reshape(1, 1, l2, cw)
        y = _dft_stage2_call(e2, z, cs, wbd, scale=scale, real_only=True)
        return y.reshape(n, cw)
    z = _dft_stage1_call(jnp.asarray(f1s, dtype=BF16), u.reshape(l1, l2 * cw))
    y = _dft_stage2_call(e2, z.reshape(2, l1, l2, cw), cs, wbd, scale=scale, real_only=False)
    return jnp.transpose(y, (1, 0, 2)).reshape(n, cw)


def _store_token_tiles(ref, x):
    n, d = x.shape
    for s in range(d // LANES):
        ref[pl.ds(s, n, stride=SUBLANES), :] = x[:, s * LANES:(s + 1) * LANES]


def _load_token_tiles(ref, n):
    parts = [ref[pl.ds(s, n, stride=SUBLANES), :] for s in range(SUBLANES)]
    return jnp.concatenate(parts, axis=1)


def _out_kernel(mix_ref, x_ref, w_ref, m_ref, g2_ref, wr_ref, br_ref,
                xo_ref, h2_ref, gmat_ref, g4_ref, i4_ref):
    o = jnp.dot(mix_ref[...], w_ref[...], preferred_element_type=F32)
    xn = x_ref[...] + m_ref[0, 2:3, :] * o
    xo_ref[...] = xn
    ms = jnp.mean(xn * xn, axis=-1, keepdims=True)
    h2 = xn * lax.rsqrt(ms + NORM_EPS) * g2_ref[...]
    h2 = h2 * (1.0 + m_ref[0, 4:5, :]) + m_ref[0, 3:4, :]
    _store_token_tiles(h2_ref, h2)
    logits = jnp.dot(h2, wr_ref[...], precision=lax.Precision.HIGHEST,
                     preferred_element_type=F32) + br_ref[...]
    iota = lax.broadcasted_iota(jnp.int32, logits.shape, 1).astype(F32)
    vals, idxs = [], []
    l = logits
    for _ in range(TOP_K):
        mx = jnp.max(l, axis=1, keepdims=True)
        ik = jnp.min(jnp.where(l == mx, iota, float(ROUTER_PAD)), axis=1, keepdims=True)
        vals.append(mx)
        idxs.append(ik)
        l = jnp.where(iota == ik, -jnp.inf, l)
    es = [jnp.exp(v - vals[0]) for v in vals]
    den = es[0] + es[1] + es[2] + es[3]
    gs = [e / den for e in es]
    gmat = jnp.zeros(logits.shape, F32)
    for ik, g in zip(idxs, gs):
        gmat = jnp.where(iota == ik, g, gmat)
    gmat_ref[...] = gmat
    lane4 = lax.broadcasted_iota(jnp.int32, g4_ref.shape, 1)
    g4 = jnp.zeros(g4_ref.shape, F32)
    i4 = jnp.zeros(g4_ref.shape, F32)
    for k in range(TOP_K):
        g4 = jnp.where(lane4 == k, gs[k], g4)
        i4 = jnp.where(lane4 == k, idxs[k], i4)
    g4_ref[...] = g4
    i4_ref[...] = i4.astype(jnp.int32)


def _out_call(mix, xsrc, x_row0, w_out, mods, g2, wr, br, n_ctx_rows):
    n, d = mix.shape
    tm = _pick(n, (256, 128, 64, 32, 16), x_row0, n_ctx_rows)
    xb0 = x_row0 // tm
    nct = n_ctx_rows // tm
    row = lambda i: (i, 0)
    const = lambda i: (0, 0)
    return pl.pallas_call(
        _out_kernel,
        grid=(n // tm,),
        in_specs=[
            pl.BlockSpec((tm, d), row),
            pl.BlockSpec((tm, d), lambda i: (i + xb0, 0)),
            pl.BlockSpec((d, d), const),
            pl.BlockSpec((1, 6, d), lambda i: (jnp.where(i < nct, 1, 0), 0, 0)),
            pl.BlockSpec((1, d), const),
            pl.BlockSpec((d, ROUTER_PAD), const),
            pl.BlockSpec((1, ROUTER_PAD), const),
        ],
        out_specs=[
            pl.BlockSpec((tm, d), row),
            pl.BlockSpec((tm * SUBLANES, LANES), row),
            pl.BlockSpec((tm, ROUTER_PAD), row),
            pl.BlockSpec((tm, TOP_K), row),
            pl.BlockSpec((tm, TOP_K), row),
        ],
        out_shape=[
            jax.ShapeDtypeStruct((n, d), F32),
            jax.ShapeDtypeStruct((n * SUBLANES, LANES), F32),
            jax.ShapeDtypeStruct((n, ROUTER_PAD), F32),
            jax.ShapeDtypeStruct((n, TOP_K), F32),
            jax.ShapeDtypeStruct((n, TOP_K), jnp.int32),
        ],
        compiler_params=_cparams(("arbitrary",)),
    )(mix, xsrc, w_out, mods, g2.reshape(1, d), wr, br)


def _rank_kernel(g_ref, rank_ref, cnt_ref, carry_sc):
    @pl.when(pl.program_id(0) == 0)
    def _():
        carry_sc[...] = jnp.zeros(carry_sc.shape, F32)

    self32 = jnp.where(g_ref[...] > 0.0, 1.0, 0.0)
    sel = self32.astype(BF16)
    tm = sel.shape[0]
    r = lax.broadcasted_iota(jnp.int32, (tm, tm), 0)
    c = lax.broadcasted_iota(jnp.int32, (tm, tm), 1)
    tri = jnp.where(c < r, 1.0, 0.0).astype(BF16)
    excl = jnp.dot(tri, sel, preferred_element_type=F32) + carry_sc[0:1, :]
    rank_ref[...] = excl.astype(jnp.int32)
    tot = carry_sc[0:1, :] + jnp.sum(self32, axis=0, keepdims=True)
    carry_sc[...] = jnp.broadcast_to(tot, carry_sc.shape)
    cnt_ref[...] = jnp.broadcast_to(tot, cnt_ref.shape).astype(jnp.int32)


def _rank_call(gmat):
    n, w = gmat.shape
    tm = _pick(n, (256, 128, 64, 32, 16))
    return pl.pallas_call(
        _rank_kernel,
        grid=(n // tm,),
        in_specs=[pl.BlockSpec((tm, w), lambda i: (i, 0))],
        out_specs=[pl.BlockSpec((tm, w), lambda i: (i, 0)),
                   pl.BlockSpec((SUBLANES, w), lambda i: (0, 0))],
        out_shape=[jax.ShapeDtypeStruct((n, w), jnp.int32),
                   jax.ShapeDtypeStruct((SUBLANES, w), jnp.int32)],
        scratch_shapes=[pltpu.VMEM((SUBLANES, w), F32)],
        compiler_params=_cparams(("arbitrary",)),
    )(gmat)


def _dispatch_kernel(dest_ref, h_ref, xs_in_ref, xs_ref, sem, *, tt):
    del xs_in_ref
    base = pl.program_id(0) * (tt * TOP_K)

    def copy(t, k):
        src = pl.multiple_of(t * SUBLANES, SUBLANES)
        dst = pl.multiple_of(dest_ref[base + t * TOP_K + k] * SUBLANES, SUBLANES)
        return pltpu.make_async_copy(h_ref.at[pl.ds(src, SUBLANES)],
                                     xs_ref.at[pl.ds(dst, SUBLANES)], sem)

    def issue(t, c):
        for k in range(TOP_K):
            copy(t, k).start()
        return c

    def drain(t, c):
        for k in range(TOP_K):
            copy(t, k).wait()
        return c

    lax.fori_loop(0, tt, issue, 0)
    lax.fori_loop(0, tt, drain, 0)


def _dispatch_call(dest_flat, h2t, n_slots):
    n = h2t.shape[0] // SUBLANES
    tt = _pick(n, (256, 128, 64, 32, 16))
    xs0 = jnp.zeros((n_slots * SUBLANES, LANES), F32)
    return pl.pallas_call(
        functools.partial(_dispatch_kernel, tt=tt),
        grid_spec=pltpu.PrefetchScalarGridSpec(
            num_scalar_prefetch=1,
            grid=(n // tt,),
            in_specs=[pl.BlockSpec((tt * SUBLANES, LANES), lambda i, dref: (i, 0)),
                      pl.BlockSpec(memory_space=pl.ANY)],
            out_specs=pl.BlockSpec(memory_space=pl.ANY),
            scratch_shapes=[pltpu.SemaphoreType.DMA(())],
        ),
        out_shape=jax.ShapeDtypeStruct(xs0.shape, F32),
        input_output_aliases={2: 0},
        compiler_params=_cparams(("arbitrary",)),
    )(dest_flat, h2t, xs0)


def _expert_kernel(be_ref, nb_ref, x_ref, wgu_ref, bgu_ref, wd_ref, bd_ref, y_ref, wgu_sc, wd_sc):
    b = pl.program_id(0)
    active = b < nb_ref[0]

    @pl.when(jnp.logical_and(active, jnp.logical_or(b == 0, be_ref[b] != be_ref[jnp.maximum(b - 1, 0)])))
    def _():
        wgu_sc[...] = wgu_ref[0, 0].astype(BF16)
        wd_sc[...] = wd_ref[0, 0].astype(BF16)

    @pl.when(active)
    def _():
        x = _load_token_tiles(x_ref, MOE_BLOCK).astype(BF16)
        gu = jnp.dot(x, wgu_sc[...], preferred_element_type=F32) + bgu_ref[0]
        glu = jnp.minimum(gu[:, :EXPERT_FF], SWIGLU_LIMIT)
        lin = jnp.clip(gu[:, EXPERT_FF:], -SWIGLU_LIMIT, SWIGLU_LIMIT)
        act = glu * jax.nn.sigmoid(SWIGLU_ALPHA * glu) * (lin + 1.0)
        y = jnp.dot(act.astype(BF16), wd_sc[...], preferred_element_type=F32) + bd_ref[0]
        _store_token_tiles(y_ref, y)

    @pl.when(pl.program_id(0) >= nb_ref[0])
    def _():
        y_ref[...] = jnp.zeros(y_ref.shape, F32)


def _expert_call(blk_e, nb_used, xs, layer, wgu, bgu, wd, bd):
    nb = xs.shape[0] // (MOE_BLOCK * SUBLANES)
    _, ne, d, ff2 = wgu.shape
    blk = lambda b, be, nbu: (jnp.minimum(b, nbu[0] - 1), 0)
    wsel = lambda b, be, nbu: (layer, be[jnp.minimum(b, nbu[0] - 1)], 0, 0)
    bsel = lambda b, be, nbu: (be[jnp.minimum(b, nbu[0] - 1)], 0, 0)
    return pl.pallas_call(
        _expert_kernel,
        grid_spec=pltpu.PrefetchScalarGridSpec(
            num_scalar_prefetch=2,
            grid=(nb,),
            in_specs=[
                pl.BlockSpec((MOE_BLOCK * SUBLANES, LANES), blk),
                pl.BlockSpec((1, 1, d, ff2), wsel),
                pl.BlockSpec((1, 1, ff2), bsel),
                pl.BlockSpec((1, 1, ff2 // 2, d), wsel),
                pl.BlockSpec((1, 1, d), bsel),
            ],
            out_specs=pl.BlockSpec((MOE_BLOCK * SUBLANES, LANES), lambda b, be, nbu: (b, 0)),
            scratch_shapes=[pltpu.VMEM((d, ff2), BF16), pltpu.VMEM((ff2 // 2, d), BF16)],
        ),
        out_shape=jax.ShapeDtypeStruct(xs.shape, F32),
        compiler_params=_cparams(("arbitrary",)),
    )(blk_e, nb_used, xs, wgu, bgu.reshape(ne, 1, ff2), wd, bd.reshape(ne, 1, d))


def _combine_kernel(dest_ref, ys_ref, x_ref, g4_ref, m_ref, gf_ref, o_ref, buf, sem, *, tm, final):
    base = pl.program_id(0) * tm * TOP_K

    def copy(t, k):
        src = pl.multiple_of(dest_ref[base + t * TOP_K + k] * SUBLANES, SUBLANES)
        dst = pl.multiple_of(t * SUBLANES, SUBLANES)
        return pltpu.make_async_copy(ys_ref.at[pl.ds(src, SUBLANES)],
                                     buf.at[k, pl.ds(dst, SUBLANES)], sem)

    def issue(t, c):
        for k in range(TOP_K):
            copy(t, k).start()
        return c

    def drain(t, c):
        for k in range(TOP_K):
            copy(t, k).wait()
        return c

    lax.fori_loop(0, tm, issue, 0)
    lax.fori_loop(0, tm, drain, 0)

    g4 = g4_ref[...]
    y = _load_token_tiles(buf.at[0], tm) * g4[:, 0:1]
    for k in range(1, TOP_K):
        y = y + _load_token_tiles(buf.at[k], tm) * g4[:, k:k + 1]
    xo = x_ref[...] + m_ref[0, 5:6, :] * y
    if final:
        ms = jnp.mean(xo * xo, axis=-1, keepdims=True)
        xo = xo * lax.rsqrt(ms + NORM_EPS) * gf_ref[...]
    o_ref[...] = xo


def _combine_call(dest_flat, ys, x, g4, mods, g_final, n_ctx_rows, final):
    n, d = x.shape
    tm = _pick(n, (256, 128, 64, 32, 16), n_ctx_rows)
    nct = n_ctx_rows // tm
    row = lambda i, dref: (i, 0)
    return pl.pallas_call(
        functools.partial(_combine_kernel, tm=tm, final=final),
        grid_spec=pltpu.PrefetchScalarGridSpec(
            num_scalar_prefetch=1,
            grid=(n // tm,),
            in_specs=[
                pl.BlockSpec(memory_space=pl.ANY),
                pl.BlockSpec((tm, d), row),
                pl.BlockSpec((tm, TOP_K), row),
                pl.BlockSpec((1, 6, d), lambda i, dref: (jnp.where(i < nct, 1, 0), 0, 0)),
                pl.BlockSpec((1, d), lambda i, dref: (0, 0)),
            ],
            out_specs=pl.BlockSpec((tm, d), row),
            scratch_shapes=[pltpu.VMEM((TOP_K, tm * SUBLANES, LANES), F32),
                            pltpu.SemaphoreType.DMA(())],
        ),
        out_shape=jax.ShapeDtypeStruct((n, d), F32),
        compiler_params=_cparams(("arbitrary",)),
    )(dest_flat, ys, x, g4, mods, g_final.reshape(1, d))


def _moe(x_mid, h2t, gmat, g4, i4, mods, n_ctx_rows, layer, wgu, bgu, wd, bd, g_final, final):
    n, d = x_mid.shape
    assert d == SUBLANES * LANES
    rank, cnt = _rank_call(gmat)
    counts = cnt[0, :N_EXPERTS]
    padded = (counts + MOE_BLOCK - 1) // MOE_BLOCK * MOE_BLOCK
    ends = jnp.cumsum(padded)
    starts = ends - padded
    dest = starts[i4] + jnp.take_along_axis(rank, i4, axis=1)
    dest_flat = dest.reshape(-1).astype(jnp.int32)
    n_slots = (n * TOP_K + MOE_BLOCK - 1) // MOE_BLOCK * MOE_BLOCK + N_EXPERTS * MOE_BLOCK
    nb = n_slots // MOE_BLOCK
    blk_start = jnp.arange(nb, dtype=jnp.int32) * MOE_BLOCK
    blk_e = jnp.minimum(jnp.sum((ends[None, :] <= blk_start[:, None]).astype(jnp.int32), axis=1),
                        N_EXPERTS - 1).astype(jnp.int32)
    nb_used = (ends[-1] // MOE_BLOCK).astype(jnp.int32).reshape(1)
    xs = _dispatch_call(dest_flat, h2t, n_slots)
    ys = _expert_call(blk_e, nb_used, xs, layer, wgu, bgu, wd, bd)
    return _combine_call(dest_flat, ys, x_mid, g4, mods, g_final, n_ctx_rows, final)


def _heads(a, nh):
    r = a.shape[0]
    return jnp.transpose(a.reshape(r, nh, HEAD_DIM), (1, 0, 2))


def _unheads(o):
    n = o.shape[-1]
    o = o.reshape(-1, HEAD_DIM, n)
    return jnp.transpose(o, (2, 0, 1)).reshape(n, -1)


def kernel(x, c, ctx, c_ctx, w_mod, b_mod, g_norm1, g_norm2, w_in, w_out, w_fnet, lambda_q1, lambda_k1, lambda_q2, lambda_k2, g_subln, g_qnorm, g_knorm, w_router, b_router, w_gate_up, b_gate_up, w_down, b_down, g_final):
    bsz, seq, d = x.shape
    assert bsz == 1
    n_ctx = ctx.shape[1]
    depth = w_mod.shape[0]
    r = n_ctx + seq

    cc = jnp.zeros((SUBLANES, d), F32).at[0].set(c[0]).at[1].set(c_ctx)
    mod_all = _mod_call(cc, w_mod, b_mod)
    tabs = _rope_tables(n_ctx, seq)
    bdq = _block_diag_mean(GQA_WIDTH, HEAD_DIM)
    bdk = _block_diag_mean(GQA_KV_WIDTH, HEAD_DIM)

    xall = jnp.concatenate([ctx[0], x[0]], axis=0)
    out = None
    for l in range(depth):
        need_ctx = l < depth - 1
        lam_init = 0.8 - 0.6 * math.exp(-0.3 * l)
        mods = mod_all[l, :2].reshape(2, 6, d)
        wl = w_in[l]
        w_ext = jnp.concatenate([
            wl,
            _swap_halves_cols(wl[:, _C_DQ:_C_DK], DIFF_QK_DIM),
            _swap_halves_cols(wl[:, _C_DK:_C_DV], DIFF_QK_DIM),
            _swap_halves_cols(wl[:, _C_GQ:_C_GK], HEAD_DIM),
            _swap_halves_cols(wl[:, _C_GK:_C_GV], HEAD_DIM),
        ], axis=1).astype(BF16)
        gq = g_qnorm[l]
        gk = g_knorm[l]
        gq_sw = jnp.concatenate([gq[HEAD_DIM // 2:], gq[:HEAD_DIM // 2]])
        gk_sw = jnp.concatenate([gk[HEAD_DIM // 2:], gk[:HEAD_DIM // 2]])
        gq2 = jnp.stack([jnp.tile(gq, N_GQA_Q_HEADS), jnp.tile(gq_sw, N_GQA_Q_HEADS)])
        gk2 = jnp.stack([jnp.tile(gk, N_GQA_KV_HEADS), jnp.tile(gk_sw, N_GQA_KV_HEADS)])

        f, dq, dk, dv, gqo, gko, gvo = _in_call(xall, g_norm1[l], mods, w_ext, tabs, gq2, gk2,
                                                 bdq, bdk, n_ctx)

        dqh = jnp.transpose(dq).reshape(N_DIFF_HEADS, HEAD_DIM, r)
        dkh = _heads(dk, N_DIFF_HEADS)
        dvh = _heads(dv, N_DIFF_HEADS)
        gqh = jnp.transpose(gqo).reshape(N_GQA_KV_HEADS, GQA_GROUP, HEAD_DIM, r)
        gkh = _heads(gko, N_GQA_KV_HEADS)
        gvh = _heads(gvo, N_GQA_KV_HEADS)
        lamp = jnp.stack([lambda_q1[l], lambda_k1[l], lambda_q2[l], lambda_k2[l]]).astype(F32)
        gsub = g_subln[l].reshape(1, HEAD_DIM)
        wbd = jnp.zeros((FOURIER_WIDTH, FOURIER_WIDTH), F32)
        for g in range(N_FOURIER_GROUPS):
            sl = slice(g * FOURIER_GROUP_DIM, (g + 1) * FOURIER_GROUP_DIM)
            wbd = wbd.at[sl, sl].set(w_fnet[l, g])
        wbd = wbd.astype(BF16)

        def mixers(q_row0, n_q, n_k):
            d_o = _attn_call(dqh, dkh, dvh, diff=True, q_row0=q_row0, n_q=n_q, n_k=n_k,
                             lamp=lamp, gsub=gsub, lam_scale=1.0 - lam_init)
            g_o = _attn_call(gqh, gkh, gvh, diff=False, q_row0=q_row0, n_q=n_q, n_k=n_k)
            f_o = _fnet_call(f[q_row0:q_row0 + n_q], wbd)
            return jnp.concatenate([f_o, _unheads(d_o), _unheads(g_o)], axis=1)

        mix = mixers(n_ctx, seq, r)
        if need_ctx:
            mix = jnp.concatenate([mixers(0, n_ctx, n_ctx), mix], axis=0)
            x_row0, nct = 0, n_ctx
        else:
            x_row0, nct = n_ctx, 0

        wr = jnp.zeros((d, ROUTER_PAD), F32).at[:, :N_EXPERTS].set(w_router[l])
        br = jnp.full((1, ROUTER_PAD), NEG_BIG, F32).at[0, :N_EXPERTS].set(b_router[l])
        x_mid, h2, gmat, g4, i4 = _out_call(mix, xall, x_row0, w_out[l].astype(BF16), mods,
                                            g_norm2[l], wr, br, nct)
        xall = _moe(x_mid, h2, gmat, g4, i4, mods, nct,
                    l, w_gate_up, b_gate_up[l], w_down, b_down[l],
                    g_final, final=not need_ctx)
    return xall.reshape(1, seq, d)
```

```python
import functools
import math

import numpy as np
import jax
import jax.numpy as jnp
from jax import lax
from jax.experimental import pallas as pl
from jax.experimental.pallas import tpu as pltpu

F32 = jnp.float32
BF16 = jnp.bfloat16

GRID_W = 64
HEAD_DIM = 64
N_FOURIER_GROUPS = 4
FOURIER_GROUP_DIM = 64
FOURIER_WIDTH = N_FOURIER_GROUPS * FOURIER_GROUP_DIM
N_DIFF_HEADS = 6
DIFF_QK_DIM = HEAD_DIM // 2
DIFF_WIDTH = N_DIFF_HEADS * HEAD_DIM
N_GQA_Q_HEADS = 6
N_GQA_KV_HEADS = 2
GQA_GROUP = N_GQA_Q_HEADS // N_GQA_KV_HEADS
GQA_WIDTH = N_GQA_Q_HEADS * HEAD_DIM
GQA_KV_WIDTH = N_GQA_KV_HEADS * HEAD_DIM
N_EXPERTS = 32
TOP_K = 4
EXPERT_FF = 1024
SWIGLU_ALPHA = 1.702
SWIGLU_LIMIT = 7.0
ROPE_THETA = 10000.0
NORM_EPS = 1e-6
SUBLN_EPS = 1e-5
LOG2E = math.log2(math.e)
DIFF_SCALE = DIFF_QK_DIM ** -0.5 * LOG2E
GQA_SCALE = HEAD_DIM ** -0.5 * LOG2E

LANES = 128
SUBLANES = 8
VMEM_LIMIT = 56 * 1024 * 1024

ROUTER_PAD = LANES
NEG_BIG = -1e30
MOE_BLOCK = 256
ATTN_TK = 1280
ATTN_SLOTS = 3
ATTN_QT = 4
ATTN_RINGS_PER_ITER = 2
FFT_L2 = 128


def _pick(n, prefs, *offsets):
    g = n
    for o in offsets:
        g = math.gcd(g, o)
    for p in prefs:
        if g % p == 0:
            return p
    return g


def _cparams(sem):
    return pltpu.CompilerParams(dimension_semantics=sem, vmem_limit_bytes=VMEM_LIMIT)


def _mod_kernel(c_ref, w_ref, b_ref, o_ref):
    c = c_ref[...]
    s = c * jax.nn.sigmoid(c)
    o_ref[0] = jnp.dot(s, w_ref[0], precision=lax.Precision.HIGHEST,
                       preferred_element_type=F32) + b_ref[0]


def _mod_call(cc, w_mod, b_mod):
    depth, d, n6 = w_mod.shape
    tn = _pick(n6, (1536, 1024, 512, 256, 128))
    return pl.pallas_call(
        _mod_kernel,
        grid=(depth, n6 // tn),
        in_specs=[
            pl.BlockSpec((SUBLANES, d), lambda l, j: (0, 0)),
            pl.BlockSpec((1, d, tn), lambda l, j: (l, 0, j)),
            pl.BlockSpec((1, 1, tn), lambda l, j: (l, 0, j)),
        ],
        out_specs=pl.BlockSpec((1, SUBLANES, tn), lambda l, j: (l, 0, j)),
        out_shape=jax.ShapeDtypeStruct((depth, SUBLANES, n6), F32),
        compiler_params=_cparams(("arbitrary", "arbitrary")),
    )(cc, w_mod, b_mod.reshape(depth, 1, n6))


_C_F = 0
_C_DQ = _C_F + FOURIER_WIDTH
_C_DK = _C_DQ + DIFF_WIDTH
_C_DV = _C_DK + DIFF_WIDTH
_C_GQ = _C_DV + DIFF_WIDTH
_C_GK = _C_GQ + GQA_WIDTH
_C_GV = _C_GK + GQA_KV_WIDTH
_C_DQS = _C_GV + GQA_KV_WIDTH
_C_DKS = _C_DQS + DIFF_WIDTH
_C_GQS = _C_DKS + DIFF_WIDTH
_C_GKS = _C_GQS + GQA_WIDTH
_C_END = _C_GKS + GQA_KV_WIDTH


def _swap_halves_cols(w, chunk):
    k, n = w.shape
    w = w.reshape(k, n // chunk, 2, chunk // 2)
    return w[:, :, ::-1, :].reshape(k, n)


def _head_mean_sq(z, bd):
    q2 = z * z
    hi = q2.astype(BF16)
    lo = (q2 - hi.astype(F32)).astype(BF16)
    return (jnp.dot(hi, bd, preferred_element_type=F32)
            + jnp.dot(lo, bd, preferred_element_type=F32))


def _in_kernel(x_ref, g_ref, m_ref, w_ref, cd_ref, sd_ref, cg_ref, sg_ref,
               gq_ref, gk_ref, bdq_ref, bdk_ref,
               f_ref, dq_ref, dk_ref, dv_ref, gqo_ref, gko_ref, gvo_ref):
    x = x_ref[...]
    ms = jnp.mean(x * x, axis=-1, keepdims=True)
    y = x * lax.rsqrt(ms + NORM_EPS) * g_ref[...]
    h = y * (1.0 + m_ref[0, 1:2, :]) + m_ref[0, 0:1, :]
    z = jnp.dot(h.astype(BF16), w_ref[...], preferred_element_type=F32)

    f_ref[...] = z[:, _C_F:_C_DQ].astype(BF16)
    dv_ref[...] = z[:, _C_DV:_C_GQ].astype(BF16)
    gvo_ref[...] = z[:, _C_GV:_C_DQS].astype(BF16)

    cd = cd_ref[...]
    sd = sd_ref[...]
    cd3 = jnp.concatenate([cd, cd, cd], axis=1)
    sd3 = jnp.concatenate([sd, sd, sd], axis=1)
    dq = z[:, _C_DQ:_C_DK] * cd3 + z[:, _C_DQS:_C_DKS] * sd3
    dq_ref[...] = (dq * DIFF_SCALE).astype(BF16)
    dk = z[:, _C_DK:_C_DV] * cd3 + z[:, _C_DKS:_C_GQS] * sd3
    dk_ref[...] = dk.astype(BF16)

    cg = cg_ref[...]
    sg = sg_ref[...]
    cg3 = jnp.concatenate([cg, cg, cg], axis=1)
    sg3 = jnp.concatenate([sg, sg, sg], axis=1)
    zq = z[:, _C_GQ:_C_GK]
    rq = lax.rsqrt(_head_mean_sq(zq, bdq_ref[...]) + NORM_EPS)
    gq = rq * (zq * (gq_ref[0:1, :] * cg3) + z[:, _C_GQS:_C_GKS] * (gq_ref[1:2, :] * sg3))
    gqo_ref[...] = (gq * GQA_SCALE).astype(BF16)
    zk = z[:, _C_GK:_C_GV]
    rk = lax.rsqrt(_head_mean_sq(zk, bdk_ref[...]) + NORM_EPS)
    gk = rk * (zk * (gk_ref[0:1, :] * cg) + z[:, _C_GKS:_C_END] * (gk_ref[1:2, :] * sg))
    gko_ref[...] = gk.astype(BF16)


def _in_call(xall, g1, mods, w_ext, tabs, gq2, gk2, bdq, bdk, n_ctx_rows):
    r, d = xall.shape
    tm = _pick(r, (256, 128, 64, 32, 16), n_ctx_rows)
    nct = n_ctx_rows // tm
    cd, sd, cg, sg = tabs
    row = lambda i: (i, 0)
    const = lambda i: (0, 0)
    widths = (FOURIER_WIDTH, DIFF_WIDTH, DIFF_WIDTH, DIFF_WIDTH, GQA_WIDTH, GQA_KV_WIDTH, GQA_KV_WIDTH)
    return pl.pallas_call(
        _in_kernel,
        grid=(r // tm,),
        in_specs=[
            pl.BlockSpec((tm, d), row),
            pl.BlockSpec((1, d), const),
            pl.BlockSpec((1, 6, d), lambda i: (jnp.where(i < nct, 1, 0), 0, 0)),
            pl.BlockSpec((d, _C_END), const),
            pl.BlockSpec((tm, LANES), row),
            pl.BlockSpec((tm, LANES), row),
            pl.BlockSpec((tm, LANES), row),
            pl.BlockSpec((tm, LANES), row),
            pl.BlockSpec((2, GQA_WIDTH), const),
            pl.BlockSpec((2, GQA_KV_WIDTH), const),
            pl.BlockSpec((GQA_WIDTH, GQA_WIDTH), const),
            pl.BlockSpec((GQA_KV_WIDTH, GQA_KV_WIDTH), const),
        ],
        out_specs=[pl.BlockSpec((tm, w), row) for w in widths],
        out_shape=[jax.ShapeDtypeStruct((r, w), BF16) for w in widths],
        compiler_params=_cparams(("arbitrary",)),
    )(xall, g1.reshape(1, d), mods, w_ext, cd, sd, cg, sg, gq2, gk2, bdq, bdk)


def _rope_tables(n_ctx, seq):
    rows = seq // GRID_W
    row_ids = jnp.repeat(jnp.arange(rows, dtype=F32), GRID_W)
    col_ids = jnp.tile(jnp.arange(GRID_W, dtype=F32), rows)

    def tab(rot_dim):
        n = rot_dim // 4
        inv = ROPE_THETA ** (-jnp.arange(n, dtype=F32) / n)
        ang = jnp.concatenate([row_ids[:, None] * inv, col_ids[:, None] * inv], axis=-1)
        cos, sin = jnp.cos(ang), jnp.sin(ang)
        reps = LANES // rot_dim
        c = jnp.tile(jnp.concatenate([cos, cos], axis=-1), (1, reps))
        s = jnp.tile(jnp.concatenate([-sin, sin], axis=-1), (1, reps))
        c = jnp.concatenate([jnp.ones((n_ctx, LANES), F32), c], axis=0)
        s = jnp.concatenate([jnp.zeros((n_ctx, LANES), F32), s], axis=0)
        return c, s

    cd, sd = tab(DIFF_QK_DIM)
    cg, sg = tab(HEAD_DIM)
    return cd, sd, cg, sg


def _block_diag_mean(width, chunk):
    a = np.kron(np.eye(width // chunk), np.full((chunk, chunk), 1.0 / chunk))
    return jnp.asarray(a, dtype=BF16)


VT_ROWS = 80


def _attn_kernel(*refs, diff, tq, n_qt, ks, spc, nk, lam_scale):
    groups = 2 if diff else GQA_GROUP
    n_in = 5 if diff else 3
    tk = spc * ks
    o_ref = refs[n_in]
    sc = list(refs[n_in + 1:])

    def take(n):
        out = sc[:n]
        del sc[:n]
        return out

    qt_sc, m_sc, acc_sc = take(groups), take(groups), take(groups)
    slots, skew = ATTN_SLOTS, ATTN_SLOTS - 1
    cmax_sc = [take(groups) for _ in range(slots)]
    s_sc = [take(groups) for _ in range(slots)]
    if diff:
        qt_ref, k_ref, vt_ref, lamp_ref, gsub_ref = refs[:n_in]
    else:
        qt_ref, k_ref, vt_ref = refs[:n_in]
    for t in range(n_qt):
        cols = pl.ds(t * tq, tq)
        if diff:
            qt = qt_ref[0, :, cols]
            row = lax.broadcasted_iota(jnp.int32, qt.shape, 0)
            zero = jnp.zeros_like(qt)
            qt_sc[0][t] = jnp.where(row < DIFF_QK_DIM, qt, zero)
            qt_sc[1][t] = jnp.where(row >= DIFF_QK_DIM, qt, zero)
        else:
            for g in range(groups):
                qt_sc[g][t] = qt_ref[0, g, :, cols]
    for g in range(groups):
        m_sc[g][...] = jnp.full(m_sc[g].shape, -jnp.inf, F32)
        acc_sc[g][...] = jnp.zeros(acc_sc[g].shape, F32)

    def key_rows(c, a):
        start = c * tk + a * ks
        if not isinstance(start, int):
            start = pl.multiple_of(start, ks)
        return k_ref[0, pl.ds(start, ks), :]

    def step(r, val, sco):
        vs, ss = r % slots, (r + skew) % slots
        rows = [pl.ds(a * ks, ks) for a in range(spc)]
        if val is not None:
            tv, cv = val
            m_new, alpha = [], []
            for g in range(groups):
                m_old = m_sc[g][tv]
                m_new.append(jnp.maximum(m_old, cmax_sc[vs][g][...]))
                alpha.append(jnp.exp2(m_old - m_new[g]))
                m_sc[g][tv] = m_new[g]
        if sco is not None:
            ts, cs = sco
        pv = [None] * groups
        cm = [None] * groups
        for a in range(spc):
            if sco is not None:
                ka = key_rows(cs, a)
                for g in range(groups):
                    st = jnp.dot(ka, qt_sc[g][ts], preferred_element_type=F32)
                    s_sc[ss][g][rows[a], :] = st
                    ca = jnp.max(st, axis=0, keepdims=True)
                    cm[g] = ca if cm[g] is None else jnp.maximum(cm[g], ca)
            if val is not None:
                for g in range(groups):
                    pt = jnp.exp2(s_sc[vs][g][rows[a], :] - m_new[g]).astype(BF16)
                    d = jnp.dot(vt_ref[0, cv * spc + a], pt, preferred_element_type=F32)
                    pv[g] = d if pv[g] is None else pv[g] + d
        for g in range(groups):
            if sco is not None:
                cmax_sc[ss][g][...] = cm[g]
            if val is not None:
                acc_sc[g][tv] = acc_sc[g][tv] * alpha[g] + pv[g]

    total = n_qt * nk
    flat = lambda f: divmod(f, nk)
    for f in range(min(skew, total)):
        step(f - skew, None, flat(f))
    n_both = max(total - skew, 0)

    def advance(t, c):
        last = c == nk - 1
        return jnp.where(last, t + 1, t), jnp.where(last, 0, c + 1)

    body_steps = slots * ATTN_RINGS_PER_ITER

    def rings(i, carry):
        tv, cv, ts, cs = carry
        for r in range(body_steps):
            step(r, (tv, cv), (ts, cs))
            tv, cv = advance(tv, cv)
            ts, cs = advance(ts, cs)
        return tv, cv, ts, cs

    if n_both // body_steps:
        init = tuple(jnp.int32(x) for x in flat(0) + flat(skew))
        lax.fori_loop(0, n_both // body_steps, rings, init)
    for f in range(n_both - n_both % body_steps, total):
        step(f, flat(f), flat(f + skew) if f < n_both else None)

    for t in range(n_qt):
        cols = pl.ds(t * tq, tq)
        outs = []
        for g in range(groups):
            acc = acc_sc[g][t]
            outs.append(acc[:HEAD_DIM] / acc[HEAD_DIM:HEAD_DIM + 1])
        if diff:
            lp = lamp_ref[...]
            lam = (jnp.exp(jnp.sum(lp[0:1] * lp[1:2], axis=1, keepdims=True))
                   - jnp.exp(jnp.sum(lp[2:3] * lp[3:4], axis=1, keepdims=True))
                   + (1.0 - lam_scale))
            od = outs[0] - lam * outs[1]
            ms = jnp.mean(od * od, axis=0, keepdims=True)
            o_ref[0, :, cols] = (od * lax.rsqrt(ms + SUBLN_EPS) * gsub_ref[...]
                                 * lam_scale).astype(o_ref.dtype)
        else:
            for g in range(groups):
                o_ref[0, g, :, cols] = outs[g].astype(o_ref.dtype)


def _attn_call(qt, k, v, *, diff, q_row0, n_q, n_k, lamp=None, gsub=None, lam_scale=1.0):
    nh = qt.shape[0]
    tq = _pick(n_q, (256, 128))
    n_qt = _pick(n_q // tq, (ATTN_QT, 2, 1))
    bq = n_qt * tq
    ks = _pick(n_k, (256, 128))
    n_sub = n_k // ks
    spc = _pick(n_sub, tuple(range(ATTN_TK // ks, 0, -1)))
    nk = n_sub // spc
    tk = spc * ks
    groups = 2 if diff else GQA_GROUP
    kernel = functools.partial(_attn_kernel, diff=diff, tq=tq, n_qt=n_qt, ks=ks, spc=spc, nk=nk,
                               lam_scale=lam_scale)
    qt = qt[..., q_row0:q_row0 + n_q]
    vk = v[:, :n_k]
    vt = jnp.concatenate([vk, jnp.ones(vk.shape[:2] + (1,), vk.dtype),
                          jnp.zeros(vk.shape[:2] + (VT_ROWS - HEAD_DIM - 1,), vk.dtype)], axis=-1)
    vt = jnp.transpose(vt.reshape(vk.shape[0], n_sub, ks, VT_ROWS), (0, 1, 3, 2))
    kv_specs = [
        pl.BlockSpec((1, n_k, HEAD_DIM), lambda h, i: (h, 0, 0)),
        pl.BlockSpec((1, n_sub, VT_ROWS, ks), lambda h, i: (h, 0, 0, 0)),
    ]
    if diff:
        in_specs = [pl.BlockSpec((1, HEAD_DIM, bq), lambda h, i: (h, 0, i))] + kv_specs + [
            pl.BlockSpec((4, DIFF_QK_DIM), lambda h, i: (0, 0)),
            pl.BlockSpec((HEAD_DIM, 1), lambda h, i: (0, 0)),
        ]
        args = (qt, k, vt, lamp, gsub.reshape(HEAD_DIM, 1))
        out_spec = pl.BlockSpec((1, HEAD_DIM, bq), lambda h, i: (h, 0, i))
        out_shape = jax.ShapeDtypeStruct((nh, HEAD_DIM, n_q), BF16)
    else:
        in_specs = [pl.BlockSpec((1, groups, HEAD_DIM, bq), lambda h, i: (h, 0, 0, i))] + kv_specs
        args = (qt, k, vt)
        out_spec = pl.BlockSpec((1, groups, HEAD_DIM, bq), lambda h, i: (h, 0, 0, i))
        out_shape = jax.ShapeDtypeStruct((nh, groups, HEAD_DIM, n_q), BF16)
    return pl.pallas_call(
        kernel,
        grid=(nh, n_q // bq),
        in_specs=in_specs,
        out_specs=out_spec,
        out_shape=out_shape,
        scratch_shapes=([pltpu.VMEM((n_qt, HEAD_DIM, tq), BF16)] * groups
                        + [pltpu.VMEM((n_qt, 1, tq), F32)] * groups
                        + [pltpu.VMEM((n_qt, VT_ROWS, tq), F32)] * groups
                        + [pltpu.VMEM((1, tq), F32)] * (ATTN_SLOTS * groups)
                        + [pltpu.VMEM((tk, tq), F32)] * (ATTN_SLOTS * groups)),
        compiler_params=_cparams(("arbitrary", "arbitrary")),
    )(*args)


def _dft_stage1_kernel(f1_ref, u_ref, z_ref):
    z = jnp.dot(f1_ref[...], u_ref[...], preferred_element_type=F32)
    l1 = u_ref.shape[0]
    z_ref[0] = z[:l1].astype(z_ref.dtype)
    z_ref[1] = z[l1:].astype(z_ref.dtype)


def _dft_stage1_call(f1s, u2d):
    l1, n = u2d.shape
    tn = _pick(n, (2048, 1024, 512, 256, 128))
    return pl.pallas_call(
        _dft_stage1_kernel,
        grid=(n // tn,),
        in_specs=[pl.BlockSpec((2 * l1, l1), lambda j: (0, 0)),
                  pl.BlockSpec((l1, tn), lambda j: (0, j))],
        out_specs=pl.BlockSpec((2, l1, tn), lambda j: (0, 0, j)),
        out_shape=jax.ShapeDtypeStruct((2, l1, n), BF16),
        compiler_params=_cparams(("arbitrary",)),
    )(f1s, u2d)


def _dft_stage2_kernel(e_ref, z_ref, cs_ref, w_ref, o_ref, *, kb, scale, real_only):
    for b in range(kb):
        zr = z_ref[0, b]
        if real_only:
            e = e_ref[b]
            x = jnp.dot(e, zr, preferred_element_type=F32)
            l2 = zr.shape[0]
            xcat = jnp.concatenate([x[:l2], x[l2:]], axis=1)
        else:
            zi = z_ref[1, b]
            rhs = jnp.concatenate(
                [jnp.concatenate([zr, zi], axis=1), jnp.concatenate([zi, -zr], axis=1)], axis=0)
            xcat = jnp.dot(e_ref[b], rhs, preferred_element_type=F32)
        y1 = jnp.dot(xcat.astype(BF16), cs_ref[...], preferred_element_type=F32)
        y = jnp.dot(y1.astype(BF16), w_ref[...], preferred_element_type=F32)
        o_ref[b] = (y * scale).astype(o_ref.dtype)


def _dft_stage2_call(e2, z, cs, wbd, *, scale, real_only):
    nz, l1, l2, cw = z.shape
    kb = _pick(l1, (8, 4, 2, 1))
    kernel = functools.partial(_dft_stage2_kernel, kb=kb, scale=scale, real_only=real_only)
    return pl.pallas_call(
        kernel,
        grid=(l1 // kb,),
        in_specs=[pl.BlockSpec((kb,) + e2.shape[1:], lambda i: (i, 0, 0)),
                  pl.BlockSpec((nz, kb, l2, cw), lambda i: (0, i, 0, 0)),
                  pl.BlockSpec(cs.shape, lambda i: (0, 0)),
                  pl.BlockSpec(wbd.shape, lambda i: (0, 0))],
        out_specs=pl.BlockSpec((kb, l2, cw), lambda i: (i, 0, 0)),
        out_shape=jax.ShapeDtypeStruct((l1, l2, cw), BF16),
        compiler_params=_cparams(("arbitrary",)),
    )(e2, z, cs, wbd)


@functools.lru_cache(maxsize=None)
def _dft_tables(n):
    l2 = FFT_L2 if n > 2 * FFT_L2 and n % FFT_L2 == 0 else n
    l1 = n // l2
    k1 = np.arange(l1)
    ang1 = 2.0 * np.pi * ((k1[:, None] * k1[None, :]) % l1) / l1
    f1s = np.concatenate([np.cos(ang1), -np.sin(ang1)], axis=0)
    k = k1[:, None] + l1 * np.arange(l2)[None, :]
    n2 = np.arange(l2)
    ang = 2.0 * np.pi * ((k[:, :, None] * n2[None, None, :]) % n) / n
    if l1 == 1:
        e2 = np.concatenate([np.cos(ang), -np.sin(ang)], axis=1)
    else:
        e2 = np.concatenate([np.cos(ang), np.sin(ang)], axis=2)
    c = np.arange(FOURIER_GROUP_DIM)
    angc = 2.0 * np.pi * ((c[:, None] * c[None, :]) % FOURIER_GROUP_DIM) / FOURIER_GROUP_DIM
    eye = np.eye(N_FOURIER_GROUPS)
    cs = np.concatenate([np.kron(eye, np.cos(angc)), np.kron(eye, np.sin(angc))], axis=0)
    return l1, l2, f1s.astype(np.float32), e2.astype(np.float32), cs.astype(np.float32)


def _fnet_call(u, wbd):
    n, cw = u.shape
    l1, l2, f1s, e2, cs = _dft_tables(n)
    scale = 1.0 / math.sqrt(n * FOURIER_GROUP_DIM)
    e2 = jnp.asarray(e2, dtype=BF16)
    cs = jnp.asarray(cs, dtype=BF16)
    if l1 == 1:
        z = u.reshape(1, 1, l2, cw)
        y = _dft_stage2_call(e2, z, cs, wbd, scale=scale, real_only=True)
        return y.reshape(n, cw)
    z = _dft_stage1_call(jnp.asarray(f1s, dtype=BF16), u.reshape(l1, l2 * cw))
    y = _dft_stage2_call(e2, z.reshape(2, l1, l2, cw), cs, wbd, scale=scale, real_only=False)
    return jnp.transpose(y, (1, 0, 2)).reshape(n, cw)


def _store_token_tiles(ref, x):
    n, d = x.shape
    for s in range(d // LANES):
        ref[pl.ds(s, n, stride=SUBLANES), :] = x[:, s * LANES:(s + 1) * LANES]


def _load_token_tiles(ref, n):
    parts = [ref[pl.ds(s, n, stride=SUBLANES), :] for s in range(SUBLANES)]
    return jnp.concatenate(parts, axis=1)


def _out_kernel(mix_ref, x_ref, w_ref, m_ref, g2_ref, wr_ref, br_ref,
                xo_ref, h2_ref, gmat_ref, g4_ref, i4_ref):
    o = jnp.dot(mix_ref[...], w_ref[...], preferred_element_type=F32)
    xn = x_ref[...] + m_ref[0, 2:3, :] * o
    xo_ref[...] = xn
    ms = jnp.mean(xn * xn, axis=-1, keepdims=True)
    h2 = xn * lax.rsqrt(ms + NORM_EPS) * g2_ref[...]
    h2 = h2 * (1.0 + m_ref[0, 4:5, :]) + m_ref[0, 3:4, :]
    _store_token_tiles(h2_ref, h2)
    logits = jnp.dot(h2, wr_ref[...], precision=lax.Precision.HIGHEST,
                     preferred_element_type=F32) + br_ref[...]
    iota = lax.broadcasted_iota(jnp.int32, logits.shape, 1).astype(F32)
    vals, idxs = [], []
    l = logits
    for _ in range(TOP_K):
        mx = jnp.max(l, axis=1, keepdims=True)
        ik = jnp.min(jnp.where(l == mx, iota, float(ROUTER_PAD)), axis=1, keepdims=True)
        vals.append(mx)
        idxs.append(ik)
        l = jnp.where(iota == ik, -jnp.inf, l)
    es = [jnp.exp(v - vals[0]) for v in vals]
    den = es[0] + es[1] + es[2] + es[3]
    gs = [e / den for e in es]
    gmat = jnp.zeros(logits.shape, F32)
    for ik, g in zip(idxs, gs):
        gmat = jnp.where(iota == ik, g, gmat)
    gmat_ref[...] = gmat
    lane4 = lax.broadcasted_iota(jnp.int32, g4_ref.shape, 1)
    g4 = jnp.zeros(g4_ref.shape, F32)
    i4 = jnp.zeros(g4_ref.shape, F32)
    for k in range(TOP_K):
        g4 = jnp.where(lane4 == k, gs[k], g4)
        i4 = jnp.where(lane4 == k, idxs[k], i4)
    g4_ref[...] = g4
    i4_ref[...] = i4.astype(jnp.int32)


def _out_call(mix, xsrc, x_row0, w_out, mods, g2, wr, br, n_ctx_rows):
    n, d = mix.shape
    tm = _pick(n, (256, 128, 64, 32, 16), x_row0, n_ctx_rows)
    xb0 = x_row0 // tm
    nct = n_ctx_rows // tm
    row = lambda i: (i, 0)
    const = lambda i: (0, 0)
    return pl.pallas_call(
        _out_kernel,
        grid=(n // tm,),
        in_specs=[
            pl.BlockSpec((tm, d), row),
            pl.BlockSpec((tm, d), lambda i: (i + xb0, 0)),
            pl.BlockSpec((d, d), const),
            pl.BlockSpec((1, 6, d), lambda i: (jnp.where(i < nct, 1, 0), 0, 0)),
            pl.BlockSpec((1, d), const),
            pl.BlockSpec((d, ROUTER_PAD), const),
            pl.BlockSpec((1, ROUTER_PAD), const),
        ],
        out_specs=[
            pl.BlockSpec((tm, d), row),
            pl.BlockSpec((tm * SUBLANES, LANES), row),
            pl.BlockSpec((tm, ROUTER_PAD), row),
            pl.BlockSpec((tm, TOP_K), row),
            pl.BlockSpec((tm, TOP_K), row),
        ],
        out_shape=[
            jax.ShapeDtypeStruct((n, d), F32),
            jax.ShapeDtypeStruct((n * SUBLANES, LANES), F32),
            jax.ShapeDtypeStruct((n, ROUTER_PAD), F32),
            jax.ShapeDtypeStruct((n, TOP_K), F32),
            jax.ShapeDtypeStruct((n, TOP_K), jnp.int32),
        ],
        compiler_params=_cparams(("arbitrary",)),
    )(mix, xsrc, w_out, mods, g2.reshape(1, d), wr, br)


def _rank_kernel(g_ref, rank_ref, cnt_ref, carry_sc):
    @pl.when(pl.program_id(0) == 0)
    def _():
        carry_sc[...] = jnp.zeros(carry_sc.shape, F32)

    self32 = jnp.where(g_ref[...] > 0.0, 1.0, 0.0)
    sel = self32.astype(BF16)
    tm = sel.shape[0]
    r = lax.broadcasted_iota(jnp.int32, (tm, tm), 0)
    c = lax.broadcasted_iota(jnp.int32, (tm, tm), 1)
    tri = jnp.where(c < r, 1.0, 0.0).astype(BF16)
    excl = jnp.dot(tri, sel, preferred_element_type=F32) + carry_sc[0:1, :]
    rank_ref[...] = excl.astype(jnp.int32)
    tot = carry_sc[0:1, :] + jnp.sum(self32, axis=0, keepdims=True)
    carry_sc[...] = jnp.broadcast_to(tot, carry_sc.shape)
    cnt_ref[...] = jnp.broadcast_to(tot, cnt_ref.shape).astype(jnp.int32)


def _rank_call(gmat):
    n, w = gmat.shape
    tm = _pick(n, (256, 128, 64, 32, 16))
    return pl.pallas_call(
        _rank_kernel,
        grid=(n // tm,),
        in_specs=[pl.BlockSpec((tm, w), lambda i: (i, 0))],
        out_specs=[pl.BlockSpec((tm, w), lambda i: (i, 0)),
                   pl.BlockSpec((SUBLANES, w), lambda i: (0, 0))],
        out_shape=[jax.ShapeDtypeStruct((n, w), jnp.int32),
                   jax.ShapeDtypeStruct((SUBLANES, w), jnp.int32)],
        scratch_shapes=[pltpu.VMEM((SUBLANES, w), F32)],
        compiler_params=_cparams(("arbitrary",)),
    )(gmat)


def _dispatch_kernel(dest_ref, h_ref, xs_in_ref, xs_ref, sem, *, tt):
    del xs_in_ref
    base = pl.program_id(0) * (tt * TOP_K)

    def copy(t, k):
        src = pl.multiple_of(t * SUBLANES, SUBLANES)
        dst = pl.multiple_of(dest_ref[base + t * TOP_K + k] * SUBLANES, SUBLANES)
        return pltpu.make_async_copy(h_ref.at[pl.ds(src, SUBLANES)],
                                     xs_ref.at[pl.ds(dst, SUBLANES)], sem)

    def issue(t, c):
        for k in range(TOP_K):
            copy(t, k).start()
        return c

    def drain(t, c):
        for k in range(TOP_K):
            copy(t, k).wait()
        return c

    lax.fori_loop(0, tt, issue, 0)
    lax.fori_loop(0, tt, drain, 0)


def _dispatch_call(dest_flat, h2t, n_slots):
    n = h2t.shape[0] // SUBLANES
    tt = _pick(n, (256, 128, 64, 32, 16))
    xs0 = jnp.zeros((n_slots * SUBLANES, LANES), F32)
    return pl.pallas_call(
        functools.partial(_dispatch_kernel, tt=tt),
        grid_spec=pltpu.PrefetchScalarGridSpec(
            num_scalar_prefetch=1,
            grid=(n // tt,),
            in_specs=[pl.BlockSpec((tt * SUBLANES, LANES), lambda i, dref: (i, 0)),
                      pl.BlockSpec(memory_space=pl.ANY)],
            out_specs=pl.BlockSpec(memory_space=pl.ANY),
            scratch_shapes=[pltpu.SemaphoreType.DMA(())],
        ),
        out_shape=jax.ShapeDtypeStruct(xs0.shape, F32),
        input_output_aliases={2: 0},
        compiler_params=_cparams(("arbitrary",)),
    )(dest_flat, h2t, xs0)


def _expert_kernel(be_ref, nb_ref, x_ref, wgu_ref, bgu_ref, wd_ref, bd_ref, y_ref, wgu_sc, wd_sc):
    b = pl.program_id(0)
    active = b < nb_ref[0]

    @pl.when(jnp.logical_and(active, jnp.logical_or(b == 0, be_ref[b] != be_ref[jnp.maximum(b - 1, 0)])))
    def _():
        wgu_sc[...] = wgu_ref[0, 0].astype(BF16)
        wd_sc[...] = wd_ref[0, 0].astype(BF16)

    @pl.when(active)
    def _():
        x = _load_token_tiles(x_ref, MOE_BLOCK).astype(BF16)
        gu = jnp.dot(x, wgu_sc[...], preferred_element_type=F32) + bgu_ref[0]
        glu = jnp.minimum(gu[:, :EXPERT_FF], SWIGLU_LIMIT)
        lin = jnp.clip(gu[:, EXPERT_FF:], -SWIGLU_LIMIT, SWIGLU_LIMIT)
        act = glu * jax.nn.sigmoid(SWIGLU_ALPHA * glu) * (lin + 1.0)
        y = jnp.dot(act.astype(BF16), wd_sc[...], preferred_element_type=F32) + bd_ref[0]
        _store_token_tiles(y_ref, y)

    @pl.when(pl.program_id(0) >= nb_ref[0])
    def _():
        y_ref[...] = jnp.zeros(y_ref.shape, F32)


def _expert_call(blk_e, nb_used, xs, layer, wgu, bgu, wd, bd):
    nb = xs.shape[0] // (MOE_BLOCK * SUBLANES)
    _, ne, d, ff2 = wgu.shape
    blk = lambda b, be, nbu: (jnp.minimum(b, nbu[0] - 1), 0)
    wsel = lambda b, be, nbu: (layer, be[jnp.minimum(b, nbu[0] - 1)], 0, 0)
    bsel = lambda b, be, nbu: (be[jnp.minimum(b, nbu[0] - 1)], 0, 0)
    return pl.pallas_call(
        _expert_kernel,
        grid_spec=pltpu.PrefetchScalarGridSpec(
            num_scalar_prefetch=2,
            grid=(nb,),
            in_specs=[
                pl.BlockSpec((MOE_BLOCK * SUBLANES, LANES), blk),
                pl.BlockSpec((1, 1, d, ff2), wsel),
                pl.BlockSpec((1, 1, ff2), bsel),
                pl.BlockSpec((1, 1, ff2 // 2, d), wsel),
                pl.BlockSpec((1, 1, d), bsel),
            ],
            out_specs=pl.BlockSpec((MOE_BLOCK * SUBLANES, LANES), lambda b, be, nbu: (b, 0)),
            scratch_shapes=[pltpu.VMEM((d, ff2), BF16), pltpu.VMEM((ff2 // 2, d), BF16)],
        ),
        out_shape=jax.ShapeDtypeStruct(xs.shape, F32),
        compiler_params=_cparams(("arbitrary",)),
    )(blk_e, nb_used, xs, wgu, bgu.reshape(ne, 1, ff2), wd, bd.reshape(ne, 1, d))


def _combine_kernel(dest_ref, ys_ref, x_ref, g4_ref, m_ref, gf_ref, o_ref, buf, sem, *, tm, final):
    i = pl.program_id(0)
    slot = i % 2

    def copy(step, sl, t, k):
        src = pl.multiple_of(dest_ref[(step * tm + t) * TOP_K + k] * SUBLANES, SUBLANES)
        dst = pl.multiple_of(t * SUBLANES, SUBLANES)
        return pltpu.make_async_copy(ys_ref.at[pl.ds(src, SUBLANES)],
                                     buf.at[sl, k, pl.ds(dst, SUBLANES)], sem.at[sl])

    def issue_rows(step, sl):
        def body(t, c):
            for k in range(TOP_K):
                copy(step, sl, t, k).start()
            return c
        lax.fori_loop(0, tm, body, 0)

    @pl.when(i == 0)
    def _():
        issue_rows(0, 0)

    @pl.when(i + 1 < pl.num_programs(0))
    def _():
        issue_rows(i + 1, 1 - slot)

    def drain(t, c):
        for k in range(TOP_K):
            copy(i, slot, t, k).wait()
        return c

    lax.fori_loop(0, tm, drain, 0)

    g4 = g4_ref[...]
    y = _load_token_tiles(buf.at[slot, 0], tm) * g4[:, 0:1]
    for k in range(1, TOP_K):
        y = y + _load_token_tiles(buf.at[slot, k], tm) * g4[:, k:k + 1]
    xo = x_ref[...] + m_ref[0, 5:6, :] * y
    if final:
        ms = jnp.mean(xo * xo, axis=-1, keepdims=True)
        xo = xo * lax.rsqrt(ms + NORM_EPS) * gf_ref[...]
    o_ref[...] = xo


def _combine_call(dest_flat, ys, x, g4, mods, g_final, n_ctx_rows, final):
    n, d = x.shape
    tm = _pick(n, (256, 128, 64, 32, 16), n_ctx_rows)
    nct = n_ctx_rows // tm
    row = lambda i, dref: (i, 0)
    return pl.pallas_call(
        functools.partial(_combine_kernel, tm=tm, final=final),
        grid_spec=pltpu.PrefetchScalarGridSpec(
            num_scalar_prefetch=1,
            grid=(n // tm,),
            in_specs=[
                pl.BlockSpec(memory_space=pl.ANY),
                pl.BlockSpec((tm, d), row),
                pl.BlockSpec((tm, TOP_K), row),
                pl.BlockSpec((1, 6, d), lambda i, dref: (jnp.where(i < nct, 1, 0), 0, 0)),
                pl.BlockSpec((1, d), lambda i, dref: (0, 0)),
            ],
            out_specs=pl.BlockSpec((tm, d), row),
            scratch_shapes=[pltpu.VMEM((2, TOP_K, tm * SUBLANES, LANES), F32),
                            pltpu.SemaphoreType.DMA((2,))],
        ),
        out_shape=jax.ShapeDtypeStruct((n, d), F32),
        compiler_params=_cparams(("arbitrary",)),
    )(dest_flat, ys, x, g4, mods, g_final.reshape(1, d))


def _moe(x_mid, h2t, gmat, g4, i4, mods, n_ctx_rows, layer, wgu, bgu, wd, bd, g_final, final):
    n, d = x_mid.shape
    assert d == SUBLANES * LANES
    rank, cnt = _rank_call(gmat)
    counts = cnt[0, :N_EXPERTS]
    padded = (counts + MOE_BLOCK - 1) // MOE_BLOCK * MOE_BLOCK
    ends = jnp.cumsum(padded)
    starts = ends - padded
    dest = starts[i4] + jnp.take_along_axis(rank, i4, axis=1)
    dest_flat = dest.reshape(-1).astype(jnp.int32)
    n_slots = (n * TOP_K + MOE_BLOCK - 1) // MOE_BLOCK * MOE_BLOCK + N_EXPERTS * MOE_BLOCK
    nb = n_slots // MOE_BLOCK
    blk_start = jnp.arange(nb, dtype=jnp.int32) * MOE_BLOCK
    blk_e = jnp.minimum(jnp.sum((ends[None, :] <= blk_start[:, None]).astype(jnp.int32), axis=1),
                        N_EXPERTS - 1).astype(jnp.int32)
    nb_used = (ends[-1] // MOE_BLOCK).astype(jnp.int32).reshape(1)
    xs = _dispatch_call(dest_flat, h2t, n_slots)
    ys = _expert_call(blk_e, nb_used, xs, layer, wgu, bgu, wd, bd)
    return _combine_call(dest_flat, ys, x_mid, g4, mods, g_final, n_ctx_rows, final)


def _heads(a, nh):
    r = a.shape[0]
    return jnp.transpose(a.reshape(r, nh, HEAD_DIM), (1, 0, 2))


def _unheads(o):
    n = o.shape[-1]
    o = o.reshape(-1, HEAD_DIM, n)
    return jnp.transpose(o, (2, 0, 1)).reshape(n, -1)


def kernel(x, c, ctx, c_ctx, w_mod, b_mod, g_norm1, g_norm2, w_in, w_out, w_fnet, lambda_q1, lambda_k1, lambda_q2, lambda_k2, g_subln, g_qnorm, g_knorm, w_router, b_router, w_gate_up, b_gate_up, w_down, b_down, g_final):
    bsz, seq, d = x.shape
    assert bsz == 1
    n_ctx = ctx.shape[1]
    depth = w_mod.shape[0]
    r = n_ctx + seq

    cc = jnp.zeros((SUBLANES, d), F32).at[0].set(c[0]).at[1].set(c_ctx)
    mod_all = _mod_call(cc, w_mod, b_mod)
    tabs = _rope_tables(n_ctx, seq)
    bdq = _block_diag_mean(GQA_WIDTH, HEAD_DIM)
    bdk = _block_diag_mean(GQA_KV_WIDTH, HEAD_DIM)

    xall = jnp.concatenate([ctx[0], x[0]], axis=0)
    out = None
    for l in range(depth):
        need_ctx = l < depth - 1
        lam_init = 0.8 - 0.6 * math.exp(-0.3 * l)
        mods = mod_all[l, :2].reshape(2, 6, d)
        wl = w_in[l]
        w_ext = jnp.concatenate([
            wl,
            _swap_halves_cols(wl[:, _C_DQ:_C_DK], DIFF_QK_DIM),
            _swap_halves_cols(wl[:, _C_DK:_C_DV], DIFF_QK_DIM),
            _swap_halves_cols(wl[:, _C_GQ:_C_GK], HEAD_DIM),
            _swap_halves_cols(wl[:, _C_GK:_C_GV], HEAD_DIM),
        ], axis=1).astype(BF16)
        gq = g_qnorm[l]
        gk = g_knorm[l]
        gq_sw = jnp.concatenate([gq[HEAD_DIM // 2:], gq[:HEAD_DIM // 2]])
        gk_sw = jnp.concatenate([gk[HEAD_DIM // 2:], gk[:HEAD_DIM // 2]])
        gq2 = jnp.stack([jnp.tile(gq, N_GQA_Q_HEADS), jnp.tile(gq_sw, N_GQA_Q_HEADS)])
        gk2 = jnp.stack([jnp.tile(gk, N_GQA_KV_HEADS), jnp.tile(gk_sw, N_GQA_KV_HEADS)])

        f, dq, dk, dv, gqo, gko, gvo = _in_call(xall, g_norm1[l], mods, w_ext, tabs, gq2, gk2,
                                                 bdq, bdk, n_ctx)

        dqh = jnp.transpose(dq).reshape(N_DIFF_HEADS, HEAD_DIM, r)
        dkh = _heads(dk, N_DIFF_HEADS)
        dvh = _heads(dv, N_DIFF_HEADS)
        gqh = jnp.transpose(gqo).reshape(N_GQA_KV_HEADS, GQA_GROUP, HEAD_DIM, r)
        gkh = _heads(gko, N_GQA_KV_HEADS)
        gvh = _heads(gvo, N_GQA_KV_HEADS)
        lamp = jnp.stack([lambda_q1[l], lambda_k1[l], lambda_q2[l], lambda_k2[l]]).astype(F32)
        gsub = g_subln[l].reshape(1, HEAD_DIM)
        wbd = jnp.zeros((FOURIER_WIDTH, FOURIER_WIDTH), F32)
        for g in range(N_FOURIER_GROUPS):
            sl = slice(g * FOURIER_GROUP_DIM, (g + 1) * FOURIER_GROUP_DIM)
            wbd = wbd.at[sl, sl].set(w_fnet[l, g])
        wbd = wbd.astype(BF16)

        def mixers(q_row0, n_q, n_k):
            d_o = _attn_call(dqh, dkh, dvh, diff=True, q_row0=q_row0, n_q=n_q, n_k=n_k,
                             lamp=lamp, gsub=gsub, lam_scale=1.0 - lam_init)
            g_o = _attn_call(gqh, gkh, gvh, diff=False, q_row0=q_row0, n_q=n_q, n_k=n_k)
            f_o = _fnet_call(f[q_row0:q_row0 + n_q], wbd)
            return jnp.concatenate([f_o, _unheads(d_o), _unheads(g_o)], axis=1)

        mix = mixers(n_ctx, seq, r)
        if need_ctx:
            mix = jnp.concatenate([mixers(0, n_ctx, n_ctx), mix], axis=0)
            x_row0, nct = 0, n_ctx
        else:
            x_row0, nct = n_ctx, 0

        wr = jnp.zeros((d, ROUTER_PAD), F32).at[:, :N_EXPERTS].set(w_router[l])
        br = jnp.full((1, ROUTER_PAD), NEG_BIG, F32).at[0, :N_EXPERTS].set(b_router[l])
        x_mid, h2, gmat, g4, i4 = _out_call(mix, xall, x_row0, w_out[l].astype(BF16), mods,
                                            g_norm2[l], wr, br, nct)
        xall = _moe(x_mid, h2, gmat, g4, i4, mods, nct,
                    l, w_gate_up, b_gate_up[l], w_down, b_down[l],
                    g_final, final=not need_ctx)
    return xall.reshape(1, seq, d)
```

```python
import functools
import math

import numpy as np
import jax
import jax.numpy as jnp
from jax import lax
from jax.experimental import pallas as pl
from jax.experimental.pallas import tpu as pltpu

F32 = jnp.float32
BF16 = jnp.bfloat16

GRID_W = 64
HEAD_DIM = 64
N_FOURIER_GROUPS = 4
FOURIER_GROUP_DIM = 64
FOURIER_WIDTH = N_FOURIER_GROUPS * FOURIER_GROUP_DIM
N_DIFF_HEADS = 6
DIFF_QK_DIM = HEAD_DIM // 2
DIFF_WIDTH = N_DIFF_HEADS * HEAD_DIM
N_GQA_Q_HEADS = 6
N_GQA_KV_HEADS = 2
GQA_GROUP = N_GQA_Q_HEADS // N_GQA_KV_HEADS
GQA_WIDTH = N_GQA_Q_HEADS * HEAD_DIM
GQA_KV_WIDTH = N_GQA_KV_HEADS * HEAD_DIM
N_EXPERTS = 32
TOP_K = 4
EXPERT_FF = 1024
SWIGLU_ALPHA = 1.702
SWIGLU_LIMIT = 7.0
ROPE_THETA = 10000.0
NORM_EPS = 1e-6
SUBLN_EPS = 1e-5
LOG2E = math.log2(math.e)
DIFF_SCALE = DIFF_QK_DIM ** -0.5 * LOG2E
GQA_SCALE = HEAD_DIM ** -0.5 * LOG2E

LANES = 128
SUBLANES = 8
VMEM_LIMIT = 56 * 1024 * 1024

ROUTER_PAD = LANES
NEG_BIG = -1e30
MOE_BLOCK = 256
DMA_LOOP_UNROLL = 4
ATTN_TK = 1280
ATTN_SLOTS = 3
ATTN_QT = 4
ATTN_RINGS_PER_ITER = 4
FFT_L2 = 128


def _pick(n, prefs, *offsets):
    g = n
    for o in offsets:
        g = math.gcd(g, o)
    for p in prefs:
        if g % p == 0:
            return p
    return g


def _cparams(sem):
    return pltpu.CompilerParams(dimension_semantics=sem, vmem_limit_bytes=VMEM_LIMIT)


def _mod_kernel(c_ref, w_ref, b_ref, o_ref):
    c = c_ref[...]
    s = c * jax.nn.sigmoid(c)
    o_ref[0] = jnp.dot(s, w_ref[0], precision=lax.Precision.HIGHEST,
                       preferred_element_type=F32) + b_ref[0]


def _mod_call(cc, w_mod, b_mod):
    depth, d, n6 = w_mod.shape
    tn = _pick(n6, (1536, 1024, 512, 256, 128))
    return pl.pallas_call(
        _mod_kernel,
        grid=(depth, n6 // tn),
        in_specs=[
            pl.BlockSpec((SUBLANES, d), lambda l, j: (0, 0)),
            pl.BlockSpec((1, d, tn), lambda l, j: (l, 0, j)),
            pl.BlockSpec((1, 1, tn), lambda l, j: (l, 0, j)),
        ],
        out_specs=pl.BlockSpec((1, SUBLANES, tn), lambda l, j: (l, 0, j)),
        out_shape=jax.ShapeDtypeStruct((depth, SUBLANES, n6), F32),
        compiler_params=_cparams(("arbitrary", "arbitrary")),
    )(cc, w_mod, b_mod.reshape(depth, 1, n6))


_C_F = 0
_C_DQ = _C_F + FOURIER_WIDTH
_C_DK = _C_DQ + DIFF_WIDTH
_C_DV = _C_DK + DIFF_WIDTH
_C_GQ = _C_DV + DIFF_WIDTH
_C_GK = _C_GQ + GQA_WIDTH
_C_GV = _C_GK + GQA_KV_WIDTH
_C_DQS = _C_GV + GQA_KV_WIDTH
_C_DKS = _C_DQS + DIFF_WIDTH
_C_GQS = _C_DKS + DIFF_WIDTH
_C_GKS = _C_GQS + GQA_WIDTH
_C_END = _C_GKS + GQA_KV_WIDTH


def _swap_halves_cols(w, chunk):
    k, n = w.shape
    w = w.reshape(k, n // chunk, 2, chunk // 2)
    return w[:, :, ::-1, :].reshape(k, n)


def _head_mean_sq(z, bd):
    q2 = z * z
    hi = q2.astype(BF16)
    lo = (q2 - hi.astype(F32)).astype(BF16)
    return (jnp.dot(hi, bd, preferred_element_type=F32)
            + jnp.dot(lo, bd, preferred_element_type=F32))


def _in_kernel(x_ref, g_ref, m_ref, w_ref, cd_ref, sd_ref, cg_ref, sg_ref,
               gq_ref, gk_ref, bdq_ref, bdk_ref,
               f_ref, dq_ref, dk_ref, dv_ref, gqo_ref, gko_ref, gvo_ref):
    x = x_ref[...]
    ms = jnp.mean(x * x, axis=-1, keepdims=True)
    y = x * lax.rsqrt(ms + NORM_EPS) * g_ref[...]
    h = y * (1.0 + m_ref[0, 1:2, :]) + m_ref[0, 0:1, :]
    z = jnp.dot(h.astype(BF16), w_ref[...], preferred_element_type=F32)

    f_ref[...] = z[:, _C_F:_C_DQ].astype(BF16)
    dv_ref[...] = z[:, _C_DV:_C_GQ].astype(BF16)
    gvo_ref[...] = z[:, _C_GV:_C_DQS].astype(BF16)

    cd = cd_ref[...]
    sd = sd_ref[...]
    cd3 = jnp.concatenate([cd, cd, cd], axis=1)
    sd3 = jnp.concatenate([sd, sd, sd], axis=1)
    dq = z[:, _C_DQ:_C_DK] * cd3 + z[:, _C_DQS:_C_DKS] * sd3
    dq_ref[...] = (dq * DIFF_SCALE).astype(BF16)
    dk = z[:, _C_DK:_C_DV] * cd3 + z[:, _C_DKS:_C_GQS] * sd3
    dk_ref[...] = dk.astype(BF16)

    cg = cg_ref[...]
    sg = sg_ref[...]
    cg3 = jnp.concatenate([cg, cg, cg], axis=1)
    sg3 = jnp.concatenate([sg, sg, sg], axis=1)
    zq = z[:, _C_GQ:_C_GK]
    rq = lax.rsqrt(_head_mean_sq(zq, bdq_ref[...]) + NORM_EPS)
    gq = rq * (zq * (gq_ref[0:1, :] * cg3) + z[:, _C_GQS:_C_GKS] * (gq_ref[1:2, :] * sg3))
    gqo_ref[...] = (gq * GQA_SCALE).astype(BF16)
    zk = z[:, _C_GK:_C_GV]
    rk = lax.rsqrt(_head_mean_sq(zk, bdk_ref[...]) + NORM_EPS)
    gk = rk * (zk * (gk_ref[0:1, :] * cg) + z[:, _C_GKS:_C_END] * (gk_ref[1:2, :] * sg))
    gko_ref[...] = gk.astype(BF16)


def _in_call(xall, g1, mods, w_ext, tabs, gq2, gk2, bdq, bdk, n_ctx_rows):
    r, d = xall.shape
    tm = _pick(r, (256, 128, 64, 32, 16), n_ctx_rows)
    nct = n_ctx_rows // tm
    cd, sd, cg, sg = tabs
    row = lambda i: (i, 0)
    const = lambda i: (0, 0)
    widths = (FOURIER_WIDTH, DIFF_WIDTH, DIFF_WIDTH, DIFF_WIDTH, GQA_WIDTH, GQA_KV_WIDTH, GQA_KV_WIDTH)
    return pl.pallas_call(
        _in_kernel,
        grid=(r // tm,),
        in_specs=[
            pl.BlockSpec((tm, d), row),
            pl.BlockSpec((1, d), const),
            pl.BlockSpec((1, 6, d), lambda i: (jnp.where(i < nct, 1, 0), 0, 0)),
            pl.BlockSpec((d, _C_END), const),
            pl.BlockSpec((tm, LANES), row),
            pl.BlockSpec((tm, LANES), row),
            pl.BlockSpec((tm, LANES), row),
            pl.BlockSpec((tm, LANES), row),
            pl.BlockSpec((2, GQA_WIDTH), const),
            pl.BlockSpec((2, GQA_KV_WIDTH), const),
            pl.BlockSpec((GQA_WIDTH, GQA_WIDTH), const),
            pl.BlockSpec((GQA_KV_WIDTH, GQA_KV_WIDTH), const),
        ],
        out_specs=[pl.BlockSpec((tm, w), row) for w in widths],
        out_shape=[jax.ShapeDtypeStruct((r, w), BF16) for w in widths],
        compiler_params=_cparams(("arbitrary",)),
    )(xall, g1.reshape(1, d), mods, w_ext, cd, sd, cg, sg, gq2, gk2, bdq, bdk)


def _rope_tables(n_ctx, seq):
    rows = seq // GRID_W
    row_ids = jnp.repeat(jnp.arange(rows, dtype=F32), GRID_W)
    col_ids = jnp.tile(jnp.arange(GRID_W, dtype=F32), rows)

    def tab(rot_dim):
        n = rot_dim // 4
        inv = ROPE_THETA ** (-jnp.arange(n, dtype=F32) / n)
        ang = jnp.concatenate([row_ids[:, None] * inv, col_ids[:, None] * inv], axis=-1)
        cos, sin = jnp.cos(ang), jnp.sin(ang)
        reps = LANES // rot_dim
        c = jnp.tile(jnp.concatenate([cos, cos], axis=-1), (1, reps))
        s = jnp.tile(jnp.concatenate([-sin, sin], axis=-1), (1, reps))
        c = jnp.concatenate([jnp.ones((n_ctx, LANES), F32), c], axis=0)
        s = jnp.concatenate([jnp.zeros((n_ctx, LANES), F32), s], axis=0)
        return c, s

    cd, sd = tab(DIFF_QK_DIM)
    cg, sg = tab(HEAD_DIM)
    return cd, sd, cg, sg


def _block_diag_mean(width, chunk):
    a = np.kron(np.eye(width // chunk), np.full((chunk, chunk), 1.0 / chunk))
    return jnp.asarray(a, dtype=BF16)


VT_ROWS = 80


def _attn_kernel(*refs, diff, tq, n_qt, ks, spc, nk, lam_scale):
    groups = 2 if diff else GQA_GROUP
    n_in = 5 if diff else 3
    tk = spc * ks
    o_ref = refs[n_in]
    sc = list(refs[n_in + 1:])

    def take(n):
        out = sc[:n]
        del sc[:n]
        return out

    qt_sc, m_sc, acc_sc = take(groups), take(groups), take(groups)
    slots, skew = ATTN_SLOTS, ATTN_SLOTS - 1
    cmax_sc = [take(groups) for _ in range(slots)]
    s_sc = [take(groups) for _ in range(slots)]
    if diff:
        qt_ref, k_ref, vt_ref, lamp_ref, gsub_ref = refs[:n_in]
    else:
        qt_ref, k_ref, vt_ref = refs[:n_in]
    for t in range(n_qt):
        cols = pl.ds(t * tq, tq)
        if diff:
            qt = qt_ref[0, :, cols]
            row = lax.broadcasted_iota(jnp.int32, qt.shape, 0)
            zero = jnp.zeros_like(qt)
            qt_sc[0][t] = jnp.where(row < DIFF_QK_DIM, qt, zero)
            qt_sc[1][t] = jnp.where(row >= DIFF_QK_DIM, qt, zero)
        else:
            for g in range(groups):
                qt_sc[g][t] = qt_ref[0, g, :, cols]
    for g in range(groups):
        m_sc[g][...] = jnp.full(m_sc[g].shape, -jnp.inf, F32)
        acc_sc[g][...] = jnp.zeros(acc_sc[g].shape, F32)

    def key_rows(c, a):
        start = c * tk + a * ks
        if not isinstance(start, int):
            start = pl.multiple_of(start, ks)
        return k_ref[0, pl.ds(start, ks), :]

    def step(r, val, sco):
        vs, ss = r % slots, (r + skew) % slots
        rows = [pl.ds(a * ks, ks) for a in range(spc)]
        if val is not None:
            tv, cv = val
            m_new, alpha = [], []
            for g in range(groups):
                m_old = m_sc[g][tv]
                m_new.append(jnp.maximum(m_old, cmax_sc[vs][g][...]))
                alpha.append(jnp.exp2(m_old - m_new[g]))
                m_sc[g][tv] = m_new[g]
        if sco is not None:
            ts, cs = sco
        pv = [None] * groups
        cm = [None] * groups
        for a in range(spc):
            if sco is not None:
                ka = key_rows(cs, a)
                for g in range(groups):
                    st = jnp.dot(ka, qt_sc[g][ts], preferred_element_type=F32)
                    s_sc[ss][g][rows[a], :] = st
                    ca = jnp.max(st, axis=0, keepdims=True)
                    cm[g] = ca if cm[g] is None else jnp.maximum(cm[g], ca)
            if val is not None:
                for g in range(groups):
                    pt = jnp.exp2(s_sc[vs][g][rows[a], :] - m_new[g]).astype(BF16)
                    d = jnp.dot(vt_ref[0, cv * spc + a], pt, preferred_element_type=F32)
                    pv[g] = d if pv[g] is None else pv[g] + d
        for g in range(groups):
            if sco is not None:
                cmax_sc[ss][g][...] = cm[g]
            if val is not None:
                acc_sc[g][tv] = acc_sc[g][tv] * alpha[g] + pv[g]

    total = n_qt * nk
    flat = lambda f: divmod(f, nk)
    for f in range(min(skew, total)):
        step(f - skew, None, flat(f))
    n_both = max(total - skew, 0)

    def advance(t, c):
        last = c == nk - 1
        return jnp.where(last, t + 1, t), jnp.where(last, 0, c + 1)

    body_steps = slots * ATTN_RINGS_PER_ITER

    def rings(i, carry):
        tv, cv, ts, cs = carry
        for r in range(body_steps):
            step(r, (tv, cv), (ts, cs))
            tv, cv = advance(tv, cv)
            ts, cs = advance(ts, cs)
        return tv, cv, ts, cs

    if n_both // body_steps:
        init = tuple(jnp.int32(x) for x in flat(0) + flat(skew))
        lax.fori_loop(0, n_both // body_steps, rings, init)
    for f in range(n_both - n_both % body_steps, total):
        step(f, flat(f), flat(f + skew) if f < n_both else None)

    for t in range(n_qt):
        cols = pl.ds(t * tq, tq)
        outs = []
        for g in range(groups):
            acc = acc_sc[g][t]
            outs.append(acc[:HEAD_DIM] / acc[HEAD_DIM:HEAD_DIM + 1])
        if diff:
            lp = lamp_ref[...]
            lam = (jnp.exp(jnp.sum(lp[0:1] * lp[1:2], axis=1, keepdims=True))
                   - jnp.exp(jnp.sum(lp[2:3] * lp[3:4], axis=1, keepdims=True))
                   + (1.0 - lam_scale))
            od = outs[0] - lam * outs[1]
            ms = jnp.mean(od * od, axis=0, keepdims=True)
            o_ref[0, :, cols] = (od * lax.rsqrt(ms + SUBLN_EPS) * gsub_ref[...]
                                 * lam_scale).astype(o_ref.dtype)
        else:
            for g in range(groups):
                o_ref[0, g, :, cols] = outs[g].astype(o_ref.dtype)


def _attn_call(qt, k, v, *, diff, q_row0, n_q, n_k, lamp=None, gsub=None, lam_scale=1.0):
    nh = qt.shape[0]
    tq = _pick(n_q, (256, 128))
    n_qt = _pick(n_q // tq, (ATTN_QT, 2, 1))
    bq = n_qt * tq
    ks = _pick(n_k, (256, 128))
    n_sub = n_k // ks
    spc = _pick(n_sub, tuple(range(ATTN_TK // ks, 0, -1)))
    nk = n_sub // spc
    tk = spc * ks
    groups = 2 if diff else GQA_GROUP
    kernel = functools.partial(_attn_kernel, diff=diff, tq=tq, n_qt=n_qt, ks=ks, spc=spc, nk=nk,
                               lam_scale=lam_scale)
    qt = qt[..., q_row0:q_row0 + n_q]
    vk = v[:, :n_k]
    vt = jnp.concatenate([vk, jnp.ones(vk.shape[:2] + (1,), vk.dtype),
                          jnp.zeros(vk.shape[:2] + (VT_ROWS - HEAD_DIM - 1,), vk.dtype)], axis=-1)
    vt = jnp.transpose(vt.reshape(vk.shape[0], n_sub, ks, VT_ROWS), (0, 1, 3, 2))
    kv_specs = [
        pl.BlockSpec((1, n_k, HEAD_DIM), lambda h, i: (h, 0, 0)),
        pl.BlockSpec((1, n_sub, VT_ROWS, ks), lambda h, i: (h, 0, 0, 0)),
    ]
    if diff:
        in_specs = [pl.BlockSpec((1, HEAD_DIM, bq), lambda h, i: (h, 0, i))] + kv_specs + [
            pl.BlockSpec((4, DIFF_QK_DIM), lambda h, i: (0, 0)),
            pl.BlockSpec((HEAD_DIM, 1), lambda h, i: (0, 0)),
        ]
        args = (qt, k, vt, lamp, gsub.reshape(HEAD_DIM, 1))
        out_spec = pl.BlockSpec((1, HEAD_DIM, bq), lambda h, i: (h, 0, i))
        out_shape = jax.ShapeDtypeStruct((nh, HEAD_DIM, n_q), BF16)
    else:
        in_specs = [pl.BlockSpec((1, groups, HEAD_DIM, bq), lambda h, i: (h, 0, 0, i))] + kv_specs
        args = (qt, k, vt)
        out_spec = pl.BlockSpec((1, groups, HEAD_DIM, bq), lambda h, i: (h, 0, 0, i))
        out_shape = jax.ShapeDtypeStruct((nh, groups, HEAD_DIM, n_q), BF16)
    return pl.pallas_call(
        kernel,
        grid=(nh, n_q // bq),
        in_specs=in_specs,
        out_specs=out_spec,
        out_shape=out_shape,
        scratch_shapes=([pltpu.VMEM((n_qt, HEAD_DIM, tq), BF16)] * groups
                        + [pltpu.VMEM((n_qt, 1, tq), F32)] * groups
                        + [pltpu.VMEM((n_qt, VT_ROWS, tq), F32)] * groups
                        + [pltpu.VMEM((1, tq), F32)] * (ATTN_SLOTS * groups)
                        + [pltpu.VMEM((tk, tq), F32)] * (ATTN_SLOTS * groups)),
        compiler_params=_cparams(("arbitrary", "arbitrary")),
    )(*args)


def _dft_stage1_kernel(f1_ref, u_ref, z_ref):
    z = jnp.dot(f1_ref[...], u_ref[...], preferred_element_type=F32)
    l1 = u_ref.shape[0]
    z_ref[0] = z[:l1].astype(z_ref.dtype)
    z_ref[1] = z[l1:].astype(z_ref.dtype)


def _dft_stage1_call(f1s, u2d):
    l1, n = u2d.shape
    tn = _pick(n, (2048, 1024, 512, 256, 128))
    return pl.pallas_call(
        _dft_stage1_kernel,
        grid=(n // tn,),
        in_specs=[pl.BlockSpec((2 * l1, l1), lambda j: (0, 0)),
                  pl.BlockSpec((l1, tn), lambda j: (0, j))],
        out_specs=pl.BlockSpec((2, l1, tn), lambda j: (0, 0, j)),
        out_shape=jax.ShapeDtypeStruct((2, l1, n), BF16),
        compiler_params=_cparams(("arbitrary",)),
    )(f1s, u2d)


def _dft_stage2_kernel(e_ref, z_ref, cs_ref, w_ref, o_ref, *, kb, scale, real_only):
    for b in range(kb):
        zr = z_ref[0, b]
        if real_only:
            e = e_ref[b]
            x = jnp.dot(e, zr, preferred_element_type=F32)
            l2 = zr.shape[0]
            xcat = jnp.concatenate([x[:l2], x[l2:]], axis=1)
        else:
            zi = z_ref[1, b]
            rhs = jnp.concatenate(
                [jnp.concatenate([zr, zi], axis=1), jnp.concatenate([zi, -zr], axis=1)], axis=0)
            xcat = jnp.dot(e_ref[b], rhs, preferred_element_type=F32)
        y1 = jnp.dot(xcat.astype(BF16), cs_ref[...], preferred_element_type=F32)
        y = jnp.dot(y1.astype(BF16), w_ref[...], preferred_element_type=F32)
        o_ref[b] = (y * scale).astype(o_ref.dtype)


def _dft_stage2_call(e2, z, cs, wbd, *, scale, real_only):
    nz, l1, l2, cw = z.shape
    kb = _pick(l1, (8, 4, 2, 1))
    kernel = functools.partial(_dft_stage2_kernel, kb=kb, scale=scale, real_only=real_only)
    return pl.pallas_call(
        kernel,
        grid=(l1 // kb,),
        in_specs=[pl.BlockSpec((kb,) + e2.shape[1:], lambda i: (i, 0, 0)),
                  pl.BlockSpec((nz, kb, l2, cw), lambda i: (0, i, 0, 0)),
                  pl.BlockSpec(cs.shape, lambda i: (0, 0)),
                  pl.BlockSpec(wbd.shape, lambda i: (0, 0))],
        out_specs=pl.BlockSpec((kb, l2, cw), lambda i: (i, 0, 0)),
        out_shape=jax.ShapeDtypeStruct((l1, l2, cw), BF16),
        compiler_params=_cparams(("arbitrary",)),
    )(e2, z, cs, wbd)


@functools.lru_cache(maxsize=None)
def _dft_tables(n):
    l2 = FFT_L2 if n > 2 * FFT_L2 and n % FFT_L2 == 0 else n
    l1 = n // l2
    k1 = np.arange(l1)
    ang1 = 2.0 * np.pi * ((k1[:, None] * k1[None, :]) % l1) / l1
    f1s = np.concatenate([np.cos(ang1), -np.sin(ang1)], axis=0)
    k = k1[:, None] + l1 * np.arange(l2)[None, :]
    n2 = np.arange(l2)
    ang = 2.0 * np.pi * ((k[:, :, None] * n2[None, None, :]) % n) / n
    if l1 == 1:
        e2 = np.concatenate([np.cos(ang), -np.sin(ang)], axis=1)
    else:
        e2 = np.concatenate([np.cos(ang), np.sin(ang)], axis=2)
    c = np.arange(FOURIER_GROUP_DIM)
    angc = 2.0 * np.pi * ((c[:, None] * c[None, :]) % FOURIER_GROUP_DIM) / FOURIER_GROUP_DIM
    eye = np.eye(N_FOURIER_GROUPS)
    cs = np.concatenate([np.kron(eye, np.cos(angc)), np.kron(eye, np.sin(angc))], axis=0)
    return l1, l2, f1s.astype(np.float32), e2.astype(np.float32), cs.astype(np.float32)


def _fnet_call(u, wbd):
    n, cw = u.shape
    l1, l2, f1s, e2, cs = _dft_tables(n)
    scale = 1.0 / math.sqrt(n * FOURIER_GROUP_DIM)
    e2 = jnp.asarray(e2, dtype=BF16)
    cs = jnp.asarray(cs, dtype=BF16)
    if l1 == 1:
        z = u.reshape(1, 1, l2, cw)
        y = _dft_stage2_call(e2, z, cs, wbd, scale=scale, real_only=True)
        return y.reshape(n, cw)
    z = _dft_stage1_call(jnp.asarray(f1s, dtype=BF16), u.reshape(l1, l2 * cw))
    y = _dft_stage2_call(e2, z.reshape(2, l1, l2, cw), cs, wbd, scale=scale, real_only=False)
    return jnp.transpose(y, (1, 0, 2)).reshape(n, cw)


def _store_token_tiles(ref, x):
    n, d = x.shape
    for s in range(d // LANES):
        ref[pl.ds(s, n, stride=SUBLANES), :] = x[:, s * LANES:(s + 1) * LANES]


def _load_token_tiles(ref, n):
    parts = [ref[pl.ds(s, n, stride=SUBLANES), :] for s in range(SUBLANES)]
    return jnp.concatenate(parts, axis=1)


def _out_kernel(mix_ref, x_ref, w_ref, m_ref, g2_ref, wr_ref, br_ref,
                xo_ref, h2_ref, gmat_ref, g4_ref, i4_ref):
    o = jnp.dot(mix_ref[...], w_ref[...], preferred_element_type=F32)
    xn = x_ref[...] + m_ref[0, 2:3, :] * o
    xo_ref[...] = xn
    ms = jnp.mean(xn * xn, axis=-1, keepdims=True)
    h2 = xn * lax.rsqrt(ms + NORM_EPS) * g2_ref[...]
    h2 = h2 * (1.0 + m_ref[0, 4:5, :]) + m_ref[0, 3:4, :]
    _store_token_tiles(h2_ref, h2)
    logits = jnp.dot(h2, wr_ref[...], precision=lax.Precision.HIGHEST,
                     preferred_element_type=F32) + br_ref[...]
    iota = lax.broadcasted_iota(jnp.int32, logits.shape, 1).astype(F32)
    vals, idxs = [], []
    l = logits
    for _ in range(TOP_K):
        mx = jnp.max(l, axis=1, keepdims=True)
        ik = jnp.min(jnp.where(l == mx, iota, float(ROUTER_PAD)), axis=1, keepdims=True)
        vals.append(mx)
        idxs.append(ik)
        l = jnp.where(iota == ik, -jnp.inf, l)
    es = [jnp.exp(v - vals[0]) for v in vals]
    den = es[0] + es[1] + es[2] + es[3]
    gs = [e / den for e in es]
    gmat = jnp.zeros(logits.shape, F32)
    for ik, g in zip(idxs, gs):
        gmat = jnp.where(iota == ik, g, gmat)
    gmat_ref[...] = gmat
    lane4 = lax.broadcasted_iota(jnp.int32, g4_ref.shape, 1)
    g4 = jnp.zeros(g4_ref.shape, F32)
    i4 = jnp.zeros(g4_ref.shape, F32)
    for k in range(TOP_K):
        g4 = jnp.where(lane4 == k, gs[k], g4)
        i4 = jnp.where(lane4 == k, idxs[k], i4)
    g4_ref[...] = g4
    i4_ref[...] = i4.astype(jnp.int32)


def _out_call(mix, xsrc, x_row0, w_out, mods, g2, wr, br, n_ctx_rows):
    n, d = mix.shape
    tm = _pick(n, (256, 128, 64, 32, 16), x_row0, n_ctx_rows)
    xb0 = x_row0 // tm
    nct = n_ctx_rows // tm
    row = lambda i: (i, 0)
    const = lambda i: (0, 0)
    return pl.pallas_call(
        _out_kernel,
        grid=(n // tm,),
        in_specs=[
            pl.BlockSpec((tm, d), row),
            pl.BlockSpec((tm, d), lambda i: (i + xb0, 0)),
            pl.BlockSpec((d, d), const),
            pl.BlockSpec((1, 6, d), lambda i: (jnp.where(i < nct, 1, 0), 0, 0)),
            pl.BlockSpec((1, d), const),
            pl.BlockSpec((d, ROUTER_PAD), const),
            pl.BlockSpec((1, ROUTER_PAD), const),
        ],
        out_specs=[
            pl.BlockSpec((tm, d), row),
            pl.BlockSpec((tm * SUBLANES, LANES), row),
            pl.BlockSpec((tm, ROUTER_PAD), row),
            pl.BlockSpec((tm, TOP_K), row),
            pl.BlockSpec((tm, TOP_K), row),
        ],
        out_shape=[
            jax.ShapeDtypeStruct((n, d), F32),
            jax.ShapeDtypeStruct((n * SUBLANES, LANES), F32),
            jax.ShapeDtypeStruct((n, ROUTER_PAD), F32),
            jax.ShapeDtypeStruct((n, TOP_K), F32),
            jax.ShapeDtypeStruct((n, TOP_K), jnp.int32),
        ],
        compiler_params=_cparams(("arbitrary",)),
    )(mix, xsrc, w_out, mods, g2.reshape(1, d), wr, br)


def _rank_kernel(g_ref, i4_ref, rank_ref, cnt_ref, carry_sc):
    @pl.when(pl.program_id(0) == 0)
    def _():
        carry_sc[...] = jnp.zeros(carry_sc.shape, F32)

    self32 = jnp.where(g_ref[...] > 0.0, 1.0, 0.0)
    sel = self32.astype(BF16)
    tm = sel.shape[0]
    r = lax.broadcasted_iota(jnp.int32, (tm, tm), 0)
    c = lax.broadcasted_iota(jnp.int32, (tm, tm), 1)
    tri = jnp.where(c < r, 1.0, 0.0).astype(BF16)
    excl = jnp.dot(tri, sel, preferred_element_type=F32) + carry_sc[0:1, :]
    lane = lax.broadcasted_iota(jnp.int32, excl.shape, 1)
    lane4 = lax.broadcasted_iota(jnp.int32, rank_ref.shape, 1)
    i4 = i4_ref[...]
    rank4 = jnp.zeros(rank_ref.shape, F32)
    for k in range(TOP_K):
        rk = jnp.sum(jnp.where(lane == i4[:, k:k + 1], excl, 0.0), axis=1, keepdims=True)
        rank4 = jnp.where(lane4 == k, rk, rank4)
    rank_ref[...] = rank4.astype(jnp.int32)
    tot = carry_sc[0:1, :] + jnp.sum(self32, axis=0, keepdims=True)
    carry_sc[...] = jnp.broadcast_to(tot, carry_sc.shape)
    cnt_ref[...] = jnp.broadcast_to(tot, cnt_ref.shape).astype(jnp.int32)


def _rank_call(gmat, i4):
    n, w = gmat.shape
    tm = _pick(n, (256, 128, 64, 32, 16))
    return pl.pallas_call(
        _rank_kernel,
        grid=(n // tm,),
        in_specs=[pl.BlockSpec((tm, w), lambda i: (i, 0)),
                  pl.BlockSpec((tm, TOP_K), lambda i: (i, 0))],
        out_specs=[pl.BlockSpec((tm, TOP_K), lambda i: (i, 0)),
                   pl.BlockSpec((SUBLANES, w), lambda i: (0, 0))],
        out_shape=[jax.ShapeDtypeStruct((n, TOP_K), jnp.int32),
                   jax.ShapeDtypeStruct((SUBLANES, w), jnp.int32)],
        scratch_shapes=[pltpu.VMEM((SUBLANES, w), F32)],
        compiler_params=_cparams(("arbitrary",)),
    )(gmat, i4)


def _dispatch_kernel(dest_ref, h_ref, xs_in_ref, xs_ref, sem, *, tt):
    del xs_in_ref
    base = pl.program_id(0) * (tt * TOP_K)

    def copy(t, k):
        src = pl.multiple_of(t * SUBLANES, SUBLANES)
        dst = pl.multiple_of(dest_ref[base + t * TOP_K + k] * SUBLANES, SUBLANES)
        return pltpu.make_async_copy(h_ref.at[pl.ds(src, SUBLANES)],
                                     xs_ref.at[pl.ds(dst, SUBLANES)], sem)

    def issue(t, c):
        for k in range(TOP_K):
            copy(t, k).start()
        return c

    def drain(t, c):
        for k in range(TOP_K):
            copy(t, k).wait()
        return c

    lax.fori_loop(0, tt, issue, 0, unroll=DMA_LOOP_UNROLL)
    lax.fori_loop(0, tt, drain, 0, unroll=DMA_LOOP_UNROLL)


def _dispatch_call(dest_flat, h2t, n_slots):
    n = h2t.shape[0] // SUBLANES
    tt = _pick(n, (256, 128, 64, 32, 16))
    xs0 = jnp.zeros((n_slots * SUBLANES, LANES), F32)
    return pl.pallas_call(
        functools.partial(_dispatch_kernel, tt=tt),
        grid_spec=pltpu.PrefetchScalarGridSpec(
            num_scalar_prefetch=1,
            grid=(n // tt,),
            in_specs=[pl.BlockSpec((tt * SUBLANES, LANES), lambda i, dref: (i, 0)),
                      pl.BlockSpec(memory_space=pl.ANY)],
            out_specs=pl.BlockSpec(memory_space=pl.ANY),
            scratch_shapes=[pltpu.SemaphoreType.DMA(())],
        ),
        out_shape=jax.ShapeDtypeStruct(xs0.shape, F32),
        input_output_aliases={2: 0},
        compiler_params=_cparams(("arbitrary",)),
    )(dest_flat, h2t, xs0)


def _expert_kernel(be_ref, nb_ref, x_ref, wgu_ref, bgu_ref, wd_ref, bd_ref, y_ref, wgu_sc, wd_sc):
    b = pl.program_id(0)
    active = b < nb_ref[0]

    @pl.when(jnp.logical_and(active, jnp.logical_or(b == 0, be_ref[b] != be_ref[jnp.maximum(b - 1, 0)])))
    def _():
        wgu_sc[...] = wgu_ref[0, 0].astype(BF16)
        wd_sc[...] = wd_ref[0, 0].astype(BF16)

    @pl.when(active)
    def _():
        x = _load_token_tiles(x_ref, MOE_BLOCK).astype(BF16)
        gu = jnp.dot(x, wgu_sc[...], preferred_element_type=F32) + bgu_ref[0]
        glu = jnp.minimum(gu[:, :EXPERT_FF], SWIGLU_LIMIT)
        lin = jnp.clip(gu[:, EXPERT_FF:], -SWIGLU_LIMIT, SWIGLU_LIMIT)
        act = glu * jax.nn.sigmoid(SWIGLU_ALPHA * glu) * (lin + 1.0)
        y = jnp.dot(act.astype(BF16), wd_sc[...], preferred_element_type=F32) + bd_ref[0]
        _store_token_tiles(y_ref, y)

    @pl.when(pl.program_id(0) >= nb_ref[0])
    def _():
        y_ref[...] = jnp.zeros(y_ref.shape, F32)


def _expert_call(blk_e, nb_used, xs, layer, wgu, bgu, wd, bd):
    nb = xs.shape[0] // (MOE_BLOCK * SUBLANES)
    _, ne, d, ff2 = wgu.shape
    blk = lambda b, be, nbu: (jnp.minimum(b, nbu[0] - 1), 0)
    wsel = lambda b, be, nbu: (layer, be[jnp.minimum(b, nbu[0] - 1)], 0, 0)
    bsel = lambda b, be, nbu: (be[jnp.minimum(b, nbu[0] - 1)], 0, 0)
    return pl.pallas_call(
        _expert_kernel,
        grid_spec=pltpu.PrefetchScalarGridSpec(
            num_scalar_prefetch=2,
            grid=(nb,),
            in_specs=[
                pl.BlockSpec((MOE_BLOCK * SUBLANES, LANES), blk),
                pl.BlockSpec((1, 1, d, ff2), wsel),
                pl.BlockSpec((1, 1, ff2), bsel),
                pl.BlockSpec((1, 1, ff2 // 2, d), wsel),
                pl.BlockSpec((1, 1, d), bsel),
            ],
            out_specs=pl.BlockSpec((MOE_BLOCK * SUBLANES, LANES), lambda b, be, nbu: (b, 0)),
            scratch_shapes=[pltpu.VMEM((d, ff2), BF16), pltpu.VMEM((ff2 // 2, d), BF16)],
        ),
        out_shape=jax.ShapeDtypeStruct(xs.shape, F32),
        compiler_params=_cparams(("arbitrary",)),
    )(blk_e, nb_used, xs, wgu, bgu.reshape(ne, 1, ff2), wd, bd.reshape(ne, 1, d))


def _combine_kernel(dest_ref, ys_ref, x_ref, g4_ref, m_ref, gf_ref, o_ref, buf, sem, *, tm, final):
    base = pl.program_id(0) * tm * TOP_K

    def copy(t, k):
        src = pl.multiple_of(dest_ref[base + t * TOP_K + k] * SUBLANES, SUBLANES)
        dst = pl.multiple_of(t * SUBLANES, SUBLANES)
        return pltpu.make_async_copy(ys_ref.at[pl.ds(src, SUBLANES)],
                                     buf.at[k, pl.ds(dst, SUBLANES)], sem)

    def issue(t, c):
        for k in range(TOP_K):
            copy(t, k).start()
        return c

    def drain(t, c):
        for k in range(TOP_K):
            copy(t, k).wait()
        return c

    lax.fori_loop(0, tm, issue, 0, unroll=DMA_LOOP_UNROLL)
    lax.fori_loop(0, tm, drain, 0, unroll=DMA_LOOP_UNROLL)

    g4 = g4_ref[...]
    y = _load_token_tiles(buf.at[0], tm) * g4[:, 0:1]
    for k in range(1, TOP_K):
        y = y + _load_token_tiles(buf.at[k], tm) * g4[:, k:k + 1]
    xo = x_ref[...] + m_ref[0, 5:6, :] * y
    if final:
        ms = jnp.mean(xo * xo, axis=-1, keepdims=True)
        xo = xo * lax.rsqrt(ms + NORM_EPS) * gf_ref[...]
    o_ref[...] = xo


def _combine_call(dest_flat, ys, x, g4, mods, g_final, n_ctx_rows, final):
    n, d = x.shape
    tm = _pick(n, (256, 128, 64, 32, 16), n_ctx_rows)
    nct = n_ctx_rows // tm
    row = lambda i, dref: (i, 0)
    return pl.pallas_call(
        functools.partial(_combine_kernel, tm=tm, final=final),
        grid_spec=pltpu.PrefetchScalarGridSpec(
            num_scalar_prefetch=1,
            grid=(n // tm,),
            in_specs=[
                pl.BlockSpec(memory_space=pl.ANY),
                pl.BlockSpec((tm, d), row),
                pl.BlockSpec((tm, TOP_K), row),
                pl.BlockSpec((1, 6, d), lambda i, dref: (jnp.where(i < nct, 1, 0), 0, 0)),
                pl.BlockSpec((1, d), lambda i, dref: (0, 0)),
            ],
            out_specs=pl.BlockSpec((tm, d), row),
            scratch_shapes=[pltpu.VMEM((TOP_K, tm * SUBLANES, LANES), F32),
                            pltpu.SemaphoreType.DMA(())],
        ),
        out_shape=jax.ShapeDtypeStruct((n, d), F32),
        compiler_params=_cparams(("arbitrary",)),
    )(dest_flat, ys, x, g4, mods, g_final.reshape(1, d))


def _moe(x_mid, h2t, gmat, g4, i4, mods, n_ctx_rows, layer, wgu, bgu, wd, bd, g_final, final):
    n, d = x_mid.shape
    assert d == SUBLANES * LANES
    rank4, cnt = _rank_call(gmat, i4)
    counts = cnt[0, :N_EXPERTS]
    padded = (counts + MOE_BLOCK - 1) // MOE_BLOCK * MOE_BLOCK
    ends = jnp.cumsum(padded)
    starts = ends - padded
    start4 = jnp.sum(jnp.where(i4[..., None] == jnp.arange(N_EXPERTS, dtype=jnp.int32), starts, 0),
                     axis=-1)
    dest_flat = (start4 + rank4).reshape(-1).astype(jnp.int32)
    n_slots = (n * TOP_K + MOE_BLOCK - 1) // MOE_BLOCK * MOE_BLOCK + N_EXPERTS * MOE_BLOCK
    nb = n_slots // MOE_BLOCK
    blk_start = jnp.arange(nb, dtype=jnp.int32) * MOE_BLOCK
    blk_e = jnp.minimum(jnp.sum((ends[None, :] <= blk_start[:, None]).astype(jnp.int32), axis=1),
                        N_EXPERTS - 1).astype(jnp.int32)
    nb_used = (ends[-1] // MOE_BLOCK).astype(jnp.int32).reshape(1)
    xs = _dispatch_call(dest_flat, h2t, n_slots)
    ys = _expert_call(blk_e, nb_used, xs, layer, wgu, bgu, wd, bd)
    return _combine_call(dest_flat, ys, x_mid, g4, mods, g_final, n_ctx_rows, final)


def _heads(a, nh):
    r = a.shape[0]
    return jnp.transpose(a.reshape(r, nh, HEAD_DIM), (1, 0, 2))


def _unheads(o):
    n = o.shape[-1]
    o = o.reshape(-1, HEAD_DIM, n)
    return jnp.transpose(o, (2, 0, 1)).reshape(n, -1)


def kernel(x, c, ctx, c_ctx, w_mod, b_mod, g_norm1, g_norm2, w_in, w_out, w_fnet, lambda_q1, lambda_k1, lambda_q2, lambda_k2, g_subln, g_qnorm, g_knorm, w_router, b_router, w_gate_up, b_gate_up, w_down, b_down, g_final):
    bsz, seq, d = x.shape
    assert bsz == 1
    n_ctx = ctx.shape[1]
    depth = w_mod.shape[0]
    r = n_ctx + seq

    cc = jnp.zeros((SUBLANES, d), F32).at[0].set(c[0]).at[1].set(c_ctx)
    mod_all = _mod_call(cc, w_mod, b_mod)
    tabs = _rope_tables(n_ctx, seq)
    bdq = _block_diag_mean(GQA_WIDTH, HEAD_DIM)
    bdk = _block_diag_mean(GQA_KV_WIDTH, HEAD_DIM)

    xall = jnp.concatenate([ctx[0], x[0]], axis=0)
    out = None
    for l in range(depth):
        need_ctx = l < depth - 1
        lam_init = 0.8 - 0.6 * math.exp(-0.3 * l)
        mods = mod_all[l, :2].reshape(2, 6, d)
        wl = w_in[l]
        w_ext = jnp.concatenate([
            wl,
            _swap_halves_cols(wl[:, _C_DQ:_C_DK], DIFF_QK_DIM),
            _swap_halves_cols(wl[:, _C_DK:_C_DV], DIFF_QK_DIM),
            _swap_halves_cols(wl[:, _C_GQ:_C_GK], HEAD_DIM),
            _swap_halves_cols(wl[:, _C_GK:_C_GV], HEAD_DIM),
        ], axis=1).astype(BF16)
        gq = g_qnorm[l]
        gk = g_knorm[l]
        gq_sw = jnp.concatenate([gq[HEAD_DIM // 2:], gq[:HEAD_DIM // 2]])
        gk_sw = jnp.concatenate([gk[HEAD_DIM // 2:], gk[:HEAD_DIM // 2]])
        gq2 = jnp.stack([jnp.tile(gq, N_GQA_Q_HEADS), jnp.tile(gq_sw, N_GQA_Q_HEADS)])
        gk2 = jnp.stack([jnp.tile(gk, N_GQA_KV_HEADS), jnp.tile(gk_sw, N_GQA_KV_HEADS)])

        f, dq, dk, dv, gqo, gko, gvo = _in_call(xall, g_norm1[l], mods, w_ext, tabs, gq2, gk2,
                                                 bdq, bdk, n_ctx)

        dqh = jnp.transpose(dq).reshape(N_DIFF_HEADS, HEAD_DIM, r)
        dkh = _heads(dk, N_DIFF_HEADS)
        dvh = _heads(dv, N_DIFF_HEADS)
        gqh = jnp.transpose(gqo).reshape(N_GQA_KV_HEADS, GQA_GROUP, HEAD_DIM, r)
        gkh = _heads(gko, N_GQA_KV_HEADS)
        gvh = _heads(gvo, N_GQA_KV_HEADS)
        lamp = jnp.stack([lambda_q1[l], lambda_k1[l], lambda_q2[l], lambda_k2[l]]).astype(F32)
        gsub = g_subln[l].reshape(1, HEAD_DIM)
        wbd = jnp.zeros((FOURIER_WIDTH, FOURIER_WIDTH), F32)
        for g in range(N_FOURIER_GROUPS):
            sl = slice(g * FOURIER_GROUP_DIM, (g + 1) * FOURIER_GROUP_DIM)
            wbd = wbd.at[sl, sl].set(w_fnet[l, g])
        wbd = wbd.astype(BF16)

        def mixers(q_row0, n_q, n_k):
            d_o = _attn_call(dqh, dkh, dvh, diff=True, q_row0=q_row0, n_q=n_q, n_k=n_k,
                             lamp=lamp, gsub=gsub, lam_scale=1.0 - lam_init)
            g_o = _attn_call(gqh, gkh, gvh, diff=False, q_row0=q_row0, n_q=n_q, n_k=n_k)
            f_o = _fnet_call(f[q_row0:q_row0 + n_q], wbd)
            return jnp.concatenate([f_o, _unheads(d_o), _unheads(g_o)], axis=1)

        mix = mixers(n_ctx, seq, r)
        if need_ctx:
            mix = jnp.concatenate([mixers(0, n_ctx, n_ctx), mix], axis=0)
            x_row0, nct = 0, n_ctx
        else:
            x_row0, nct = n_ctx, 0

        wr = jnp.zeros((d, ROUTER_PAD), F32).at[:, :N_EXPERTS].set(w_router[l])
        br = jnp.full((1, ROUTER_PAD), NEG_BIG, F32).at[0, :N_EXPERTS].set(b_router[l])
        x_mid, h2, gmat, g4, i4 = _out_call(mix, xall, x_row0, w_out[l].astype(BF16), mods,
                                            g_norm2[l], wr, br, nct)
        xall = _moe(x_mid, h2, gmat, g4, i4, mods, nct,
                    l, w_gate_up, b_gate_up[l], w_down, b_down[l],
                    g_final, final=not need_ctx)
    return xall.reshape(1, seq, d)
```

```python
import functools
import math

import numpy as np
import jax
import jax.numpy as jnp
from jax import lax
from jax.experimental import pallas as pl
from jax.experimental.pallas import tpu as pltpu

F32 = jnp.float32
BF16 = jnp.bfloat16

GRID_W = 64
HEAD_DIM = 64
N_FOURIER_GROUPS = 4
FOURIER_GROUP_DIM = 64
FOURIER_WIDTH = N_FOURIER_GROUPS * FOURIER_GROUP_DIM
N_DIFF_HEADS = 6
DIFF_QK_DIM = HEAD_DIM // 2
DIFF_WIDTH = N_DIFF_HEADS * HEAD_DIM
N_GQA_Q_HEADS = 6
N_GQA_KV_HEADS = 2
GQA_GROUP = N_GQA_Q_HEADS // N_GQA_KV_HEADS
GQA_WIDTH = N_GQA_Q_HEADS * HEAD_DIM
GQA_KV_WIDTH = N_GQA_KV_HEADS * HEAD_DIM
N_EXPERTS = 32
TOP_K = 4
EXPERT_FF = 1024
SWIGLU_ALPHA = 1.702
SWIGLU_LIMIT = 7.0
ROPE_THETA = 10000.0
NORM_EPS = 1e-6
SUBLN_EPS = 1e-5
LOG2E = math.log2(math.e)
DIFF_SCALE = DIFF_QK_DIM ** -0.5 * LOG2E
GQA_SCALE = HEAD_DIM ** -0.5 * LOG2E

LANES = 128
SUBLANES = 8
VMEM_LIMIT = 56 * 1024 * 1024

ROUTER_PAD = LANES
NEG_BIG = -1e30
MOE_BLOCK = 256
DMA_LOOP_UNROLL = 4
DMA_PRIORITIES = 2
ATTN_TK = 1280
ATTN_SLOTS = 3
ATTN_QT = 4
ATTN_RINGS_PER_ITER = 4
FFT_L2 = 128


def _pick(n, prefs, *offsets):
    g = n
    for o in offsets:
        g = math.gcd(g, o)
    for p in prefs:
        if g % p == 0:
            return p
    return g


def _cparams(sem):
    return pltpu.CompilerParams(dimension_semantics=sem, vmem_limit_bytes=VMEM_LIMIT)


def _mod_kernel(c_ref, w_ref, b_ref, o_ref):
    c = c_ref[...]
    s = c * jax.nn.sigmoid(c)
    o_ref[0] = jnp.dot(s, w_ref[0], precision=lax.Precision.HIGHEST,
                       preferred_element_type=F32) + b_ref[0]


def _mod_call(cc, w_mod, b_mod):
    depth, d, n6 = w_mod.shape
    tn = _pick(n6, (1536, 1024, 512, 256, 128))
    return pl.pallas_call(
        _mod_kernel,
        grid=(depth, n6 // tn),
        in_specs=[
            pl.BlockSpec((SUBLANES, d), lambda l, j: (0, 0)),
            pl.BlockSpec((1, d, tn), lambda l, j: (l, 0, j)),
            pl.BlockSpec((1, 1, tn), lambda l, j: (l, 0, j)),
        ],
        out_specs=pl.BlockSpec((1, SUBLANES, tn), lambda l, j: (l, 0, j)),
        out_shape=jax.ShapeDtypeStruct((depth, SUBLANES, n6), F32),
        compiler_params=_cparams(("arbitrary", "arbitrary")),
    )(cc, w_mod, b_mod.reshape(depth, 1, n6))


_C_F = 0
_C_DQ = _C_F + FOURIER_WIDTH
_C_DK = _C_DQ + DIFF_WIDTH
_C_DV = _C_DK + DIFF_WIDTH
_C_GQ = _C_DV + DIFF_WIDTH
_C_GK = _C_GQ + GQA_WIDTH
_C_GV = _C_GK + GQA_KV_WIDTH
_C_DQS = _C_GV + GQA_KV_WIDTH
_C_DKS = _C_DQS + DIFF_WIDTH
_C_GQS = _C_DKS + DIFF_WIDTH
_C_GKS = _C_GQS + GQA_WIDTH
_C_END = _C_GKS + GQA_KV_WIDTH


def _swap_halves_cols(w, chunk):
    k, n = w.shape
    w = w.reshape(k, n // chunk, 2, chunk // 2)
    return w[:, :, ::-1, :].reshape(k, n)


def _head_mean_sq(z, bd):
    q2 = z * z
    hi = q2.astype(BF16)
    lo = (q2 - hi.astype(F32)).astype(BF16)
    return (jnp.dot(hi, bd, preferred_element_type=F32)
            + jnp.dot(lo, bd, preferred_element_type=F32))


def _in_kernel(x_ref, g_ref, m_ref, w_ref, cd_ref, sd_ref, cg_ref, sg_ref,
               gq_ref, gk_ref, bdq_ref, bdk_ref,
               f_ref, dq_ref, dk_ref, dv_ref, gqo_ref, gko_ref, gvo_ref):
    x = x_ref[...]
    ms = jnp.mean(x * x, axis=-1, keepdims=True)
    y = x * lax.rsqrt(ms + NORM_EPS) * g_ref[...]
    h = y * (1.0 + m_ref[0, 1:2, :]) + m_ref[0, 0:1, :]
    z = jnp.dot(h.astype(BF16), w_ref[...], preferred_element_type=F32)

    f_ref[...] = z[:, _C_F:_C_DQ].astype(BF16)
    dv_ref[...] = z[:, _C_DV:_C_GQ].astype(BF16)
    gvo_ref[...] = z[:, _C_GV:_C_DQS].astype(BF16)

    cd = cd_ref[...]
    sd = sd_ref[...]
    cd3 = jnp.concatenate([cd, cd, cd], axis=1)
    sd3 = jnp.concatenate([sd, sd, sd], axis=1)
    dq = z[:, _C_DQ:_C_DK] * cd3 + z[:, _C_DQS:_C_DKS] * sd3
    dq_ref[...] = (dq * DIFF_SCALE).astype(BF16)
    dk = z[:, _C_DK:_C_DV] * cd3 + z[:, _C_DKS:_C_GQS] * sd3
    dk_ref[...] = dk.astype(BF16)

    cg = cg_ref[...]
    sg = sg_ref[...]
    cg3 = jnp.concatenate([cg, cg, cg], axis=1)
    sg3 = jnp.concatenate([sg, sg, sg], axis=1)
    zq = z[:, _C_GQ:_C_GK]
    rq = lax.rsqrt(_head_mean_sq(zq, bdq_ref[...]) + NORM_EPS)
    gq = rq * (zq * (gq_ref[0:1, :] * cg3) + z[:, _C_GQS:_C_GKS] * (gq_ref[1:2, :] * sg3))
    gqo_ref[...] = (gq * GQA_SCALE).astype(BF16)
    zk = z[:, _C_GK:_C_GV]
    rk = lax.rsqrt(_head_mean_sq(zk, bdk_ref[...]) + NORM_EPS)
    gk = rk * (zk * (gk_ref[0:1, :] * cg) + z[:, _C_GKS:_C_END] * (gk_ref[1:2, :] * sg))
    gko_ref[...] = gk.astype(BF16)


def _in_call(xall, g1, mods, w_ext, tabs, gq2, gk2, bdq, bdk, n_ctx_rows):
    r, d = xall.shape
    tm = _pick(r, (256, 128, 64, 32, 16), n_ctx_rows)
    nct = n_ctx_rows // tm
    cd, sd, cg, sg = tabs
    row = lambda i: (i, 0)
    const = lambda i: (0, 0)
    widths = (FOURIER_WIDTH, DIFF_WIDTH, DIFF_WIDTH, DIFF_WIDTH, GQA_WIDTH, GQA_KV_WIDTH, GQA_KV_WIDTH)
    return pl.pallas_call(
        _in_kernel,
        grid=(r // tm,),
        in_specs=[
            pl.BlockSpec((tm, d), row),
            pl.BlockSpec((1, d), const),
            pl.BlockSpec((1, 6, d), lambda i: (jnp.where(i < nct, 1, 0), 0, 0)),
            pl.BlockSpec((d, _C_END), const),
            pl.BlockSpec((tm, LANES), row),
            pl.BlockSpec((tm, LANES), row),
            pl.BlockSpec((tm, LANES), row),
            pl.BlockSpec((tm, LANES), row),
            pl.BlockSpec((2, GQA_WIDTH), const),
            pl.BlockSpec((2, GQA_KV_WIDTH), const),
            pl.BlockSpec((GQA_WIDTH, GQA_WIDTH), const),
            pl.BlockSpec((GQA_KV_WIDTH, GQA_KV_WIDTH), const),
        ],
        out_specs=[pl.BlockSpec((tm, w), row) for w in widths],
        out_shape=[jax.ShapeDtypeStruct((r, w), BF16) for w in widths],
        compiler_params=_cparams(("arbitrary",)),
    )(xall, g1.reshape(1, d), mods, w_ext, cd, sd, cg, sg, gq2, gk2, bdq, bdk)


def _rope_tables(n_ctx, seq):
    rows = seq // GRID_W
    row_ids = jnp.repeat(jnp.arange(rows, dtype=F32), GRID_W)
    col_ids = jnp.tile(jnp.arange(GRID_W, dtype=F32), rows)

    def tab(rot_dim):
        n = rot_dim // 4
        inv = ROPE_THETA ** (-jnp.arange(n, dtype=F32) / n)
        ang = jnp.concatenate([row_ids[:, None] * inv, col_ids[:, None] * inv], axis=-1)
        cos, sin = jnp.cos(ang), jnp.sin(ang)
        reps = LANES // rot_dim
        c = jnp.tile(jnp.concatenate([cos, cos], axis=-1), (1, reps))
        s = jnp.tile(jnp.concatenate([-sin, sin], axis=-1), (1, reps))
        c = jnp.concatenate([jnp.ones((n_ctx, LANES), F32), c], axis=0)
        s = jnp.concatenate([jnp.zeros((n_ctx, LANES), F32), s], axis=0)
        return c, s

    cd, sd = tab(DIFF_QK_DIM)
    cg, sg = tab(HEAD_DIM)
    return cd, sd, cg, sg


def _block_diag_mean(width, chunk):
    a = np.kron(np.eye(width // chunk), np.full((chunk, chunk), 1.0 / chunk))
    return jnp.asarray(a, dtype=BF16)


VT_ROWS = 80


def _attn_kernel(*refs, diff, tq, n_qt, ks, spc, nk, lam_scale):
    groups = 2 if diff else GQA_GROUP
    n_in = 5 if diff else 3
    tk = spc * ks
    o_ref = refs[n_in]
    sc = list(refs[n_in + 1:])

    def take(n):
        out = sc[:n]
        del sc[:n]
        return out

    qt_sc, m_sc, acc_sc = take(groups), take(groups), take(groups)
    slots, skew = ATTN_SLOTS, ATTN_SLOTS - 1
    cmax_sc = [take(groups) for _ in range(slots)]
    s_sc = [take(groups) for _ in range(slots)]
    if diff:
        qt_ref, k_ref, vt_ref, lamp_ref, gsub_ref = refs[:n_in]
    else:
        qt_ref, k_ref, vt_ref = refs[:n_in]
    for t in range(n_qt):
        cols = pl.ds(t * tq, tq)
        if diff:
            qt = qt_ref[0, :, cols]
            row = lax.broadcasted_iota(jnp.int32, qt.shape, 0)
            zero = jnp.zeros_like(qt)
            qt_sc[0][t] = jnp.where(row < DIFF_QK_DIM, qt, zero)
            qt_sc[1][t] = jnp.where(row >= DIFF_QK_DIM, qt, zero)
        else:
            for g in range(groups):
                qt_sc[g][t] = qt_ref[0, g, :, cols]
    for g in range(groups):
        m_sc[g][...] = jnp.full(m_sc[g].shape, -jnp.inf, F32)
        acc_sc[g][...] = jnp.zeros(acc_sc[g].shape, F32)

    def key_rows(c, a):
        start = c * tk + a * ks
        if not isinstance(start, int):
            start = pl.multiple_of(start, ks)
        return k_ref[0, pl.ds(start, ks), :]

    def step(r, val, sco):
        vs, ss = r % slots, (r + skew) % slots
        rows = [pl.ds(a * ks, ks) for a in range(spc)]
        if val is not None:
            tv, cv = val
            m_new, alpha = [], []
            for g in range(groups):
                m_old = m_sc[g][tv]
                m_new.append(jnp.maximum(m_old, cmax_sc[vs][g][...]))
                alpha.append(jnp.exp2(m_old - m_new[g]))
                m_sc[g][tv] = m_new[g]
        if sco is not None:
            ts, cs = sco
        pv = [None] * groups
        cm = [None] * groups
        for a in range(spc):
            if sco is not None:
                ka = key_rows(cs, a)
                for g in range(groups):
                    st = jnp.dot(ka, qt_sc[g][ts], preferred_element_type=F32)
                    s_sc[ss][g][rows[a], :] = st
                    ca = jnp.max(st, axis=0, keepdims=True)
                    cm[g] = ca if cm[g] is None else jnp.maximum(cm[g], ca)
            if val is not None:
                for g in range(groups):
                    pt = jnp.exp2(s_sc[vs][g][rows[a], :] - m_new[g]).astype(BF16)
                    d = jnp.dot(vt_ref[0, cv * spc + a], pt, preferred_element_type=F32)
                    pv[g] = d if pv[g] is None else pv[g] + d
        for g in range(groups):
            if sco is not None:
                cmax_sc[ss][g][...] = cm[g]
            if val is not None:
                acc_sc[g][tv] = acc_sc[g][tv] * alpha[g] + pv[g]

    total = n_qt * nk
    flat = lambda f: divmod(f, nk)
    for f in range(min(skew, total)):
        step(f - skew, None, flat(f))
    n_both = max(total - skew, 0)

    def advance(t, c):
        last = c == nk - 1
        return jnp.where(last, t + 1, t), jnp.where(last, 0, c + 1)

    body_steps = slots * ATTN_RINGS_PER_ITER

    def rings(i, carry):
        tv, cv, ts, cs = carry
        for r in range(body_steps):
            step(r, (tv, cv), (ts, cs))
            tv, cv = advance(tv, cv)
            ts, cs = advance(ts, cs)
        return tv, cv, ts, cs

    if n_both // body_steps:
        init = tuple(jnp.int32(x) for x in flat(0) + flat(skew))
        lax.fori_loop(0, n_both // body_steps, rings, init)
    for f in range(n_both - n_both % body_steps, total):
        step(f, flat(f), flat(f + skew) if f < n_both else None)

    for t in range(n_qt):
        cols = pl.ds(t * tq, tq)
        outs = []
        for g in range(groups):
            acc = acc_sc[g][t]
            outs.append(acc[:HEAD_DIM] / acc[HEAD_DIM:HEAD_DIM + 1])
        if diff:
            lp = lamp_ref[...]
            lam = (jnp.exp(jnp.sum(lp[0:1] * lp[1:2], axis=1, keepdims=True))
                   - jnp.exp(jnp.sum(lp[2:3] * lp[3:4], axis=1, keepdims=True))
                   + (1.0 - lam_scale))
            od = outs[0] - lam * outs[1]
            ms = jnp.mean(od * od, axis=0, keepdims=True)
            o_ref[0, :, cols] = (od * lax.rsqrt(ms + SUBLN_EPS) * gsub_ref[...]
                                 * lam_scale).astype(o_ref.dtype)
        else:
            for g in range(groups):
                o_ref[0, g, :, cols] = outs[g].astype(o_ref.dtype)


def _attn_call(qt, k, v, *, diff, q_row0, n_q, n_k, lamp=None, gsub=None, lam_scale=1.0):
    nh = qt.shape[0]
    tq = _pick(n_q, (256, 128))
    n_qt = _pick(n_q // tq, (ATTN_QT, 2, 1))
    bq = n_qt * tq
    ks = _pick(n_k, (256, 128))
    n_sub = n_k // ks
    spc = _pick(n_sub, tuple(range(ATTN_TK // ks, 0, -1)))
    nk = n_sub // spc
    tk = spc * ks
    groups = 2 if diff else GQA_GROUP
    kernel = functools.partial(_attn_kernel, diff=diff, tq=tq, n_qt=n_qt, ks=ks, spc=spc, nk=nk,
                               lam_scale=lam_scale)
    qt = qt[..., q_row0:q_row0 + n_q]
    vk = v[:, :n_k]
    vt = jnp.concatenate([vk, jnp.ones(vk.shape[:2] + (1,), vk.dtype),
                          jnp.zeros(vk.shape[:2] + (VT_ROWS - HEAD_DIM - 1,), vk.dtype)], axis=-1)
    vt = jnp.transpose(vt.reshape(vk.shape[0], n_sub, ks, VT_ROWS), (0, 1, 3, 2))
    kv_specs = [
        pl.BlockSpec((1, n_k, HEAD_DIM), lambda h, i: (h, 0, 0)),
        pl.BlockSpec((1, n_sub, VT_ROWS, ks), lambda h, i: (h, 0, 0, 0)),
    ]
    if diff:
        in_specs = [pl.BlockSpec((1, HEAD_DIM, bq), lambda h, i: (h, 0, i))] + kv_specs + [
            pl.BlockSpec((4, DIFF_QK_DIM), lambda h, i: (0, 0)),
            pl.BlockSpec((HEAD_DIM, 1), lambda h, i: (0, 0)),
        ]
        args = (qt, k, vt, lamp, gsub.reshape(HEAD_DIM, 1))
        out_spec = pl.BlockSpec((1, HEAD_DIM, bq), lambda h, i: (h, 0, i))
        out_shape = jax.ShapeDtypeStruct((nh, HEAD_DIM, n_q), BF16)
    else:
        in_specs = [pl.BlockSpec((1, groups, HEAD_DIM, bq), lambda h, i: (h, 0, 0, i))] + kv_specs
        args = (qt, k, vt)
        out_spec = pl.BlockSpec((1, groups, HEAD_DIM, bq), lambda h, i: (h, 0, 0, i))
        out_shape = jax.ShapeDtypeStruct((nh, groups, HEAD_DIM, n_q), BF16)
    return pl.pallas_call(
        kernel,
        grid=(nh, n_q // bq),
        in_specs=in_specs,
        out_specs=out_spec,
        out_shape=out_shape,
        scratch_shapes=([pltpu.VMEM((n_qt, HEAD_DIM, tq), BF16)] * groups
                        + [pltpu.VMEM((n_qt, 1, tq), F32)] * groups
                        + [pltpu.VMEM((n_qt, VT_ROWS, tq), F32)] * groups
                        + [pltpu.VMEM((1, tq), F32)] * (ATTN_SLOTS * groups)
                        + [pltpu.VMEM((tk, tq), F32)] * (ATTN_SLOTS * groups)),
        compiler_params=_cparams(("arbitrary", "arbitrary")),
    )(*args)


def _dft_stage1_kernel(f1_ref, u_ref, z_ref):
    z = jnp.dot(f1_ref[...], u_ref[...], preferred_element_type=F32)
    l1 = u_ref.shape[0]
    z_ref[0] = z[:l1].astype(z_ref.dtype)
    z_ref[1] = z[l1:].astype(z_ref.dtype)


def _dft_stage1_call(f1s, u2d):
    l1, n = u2d.shape
    tn = _pick(n, (2048, 1024, 512, 256, 128))
    return pl.pallas_call(
        _dft_stage1_kernel,
        grid=(n // tn,),
        in_specs=[pl.BlockSpec((2 * l1, l1), lambda j: (0, 0)),
                  pl.BlockSpec((l1, tn), lambda j: (0, j))],
        out_specs=pl.BlockSpec((2, l1, tn), lambda j: (0, 0, j)),
        out_shape=jax.ShapeDtypeStruct((2, l1, n), BF16),
        compiler_params=_cparams(("arbitrary",)),
    )(f1s, u2d)


def _dft_stage2_kernel(e_ref, z_ref, cs_ref, w_ref, o_ref, *, kb, scale, real_only):
    for b in range(kb):
        zr = z_ref[0, b]
        if real_only:
            e = e_ref[b]
            x = jnp.dot(e, zr, preferred_element_type=F32)
            l2 = zr.shape[0]
            xcat = jnp.concatenate([x[:l2], x[l2:]], axis=1)
        else:
            zi = z_ref[1, b]
            rhs = jnp.concatenate(
                [jnp.concatenate([zr, zi], axis=1), jnp.concatenate([zi, -zr], axis=1)], axis=0)
            xcat = jnp.dot(e_ref[b], rhs, preferred_element_type=F32)
        y1 = jnp.dot(xcat.astype(BF16), cs_ref[...], preferred_element_type=F32)
        y = jnp.dot(y1.astype(BF16), w_ref[...], preferred_element_type=F32)
        o_ref[b] = (y * scale).astype(o_ref.dtype)


def _dft_stage2_call(e2, z, cs, wbd, *, scale, real_only):
    nz, l1, l2, cw = z.shape
    kb = _pick(l1, (8, 4, 2, 1))
    kernel = functools.partial(_dft_stage2_kernel, kb=kb, scale=scale, real_only=real_only)
    return pl.pallas_call(
        kernel,
        grid=(l1 // kb,),
        in_specs=[pl.BlockSpec((kb,) + e2.shape[1:], lambda i: (i, 0, 0)),
                  pl.BlockSpec((nz, kb, l2, cw), lambda i: (0, i, 0, 0)),
                  pl.BlockSpec(cs.shape, lambda i: (0, 0)),
                  pl.BlockSpec(wbd.shape, lambda i: (0, 0))],
        out_specs=pl.BlockSpec((kb, l2, cw), lambda i: (i, 0, 0)),
        out_shape=jax.ShapeDtypeStruct((l1, l2, cw), BF16),
        compiler_params=_cparams(("arbitrary",)),
    )(e2, z, cs, wbd)


@functools.lru_cache(maxsize=None)
def _dft_tables(n):
    l2 = FFT_L2 if n > 2 * FFT_L2 and n % FFT_L2 == 0 else n
    l1 = n // l2
    k1 = np.arange(l1)
    ang1 = 2.0 * np.pi * ((k1[:, None] * k1[None, :]) % l1) / l1
    f1s = np.concatenate([np.cos(ang1), -np.sin(ang1)], axis=0)
    k = k1[:, None] + l1 * np.arange(l2)[None, :]
    n2 = np.arange(l2)
    ang = 2.0 * np.pi * ((k[:, :, None] * n2[None, None, :]) % n) / n
    if l1 == 1:
        e2 = np.concatenate([np.cos(ang), -np.sin(ang)], axis=1)
    else:
        e2 = np.concatenate([np.cos(ang), np.sin(ang)], axis=2)
    c = np.arange(FOURIER_GROUP_DIM)
    angc = 2.0 * np.pi * ((c[:, None] * c[None, :]) % FOURIER_GROUP_DIM) / FOURIER_GROUP_DIM
    eye = np.eye(N_FOURIER_GROUPS)
    cs = np.concatenate([np.kron(eye, np.cos(angc)), np.kron(eye, np.sin(angc))], axis=0)
    return l1, l2, f1s.astype(np.float32), e2.astype(np.float32), cs.astype(np.float32)


def _fnet_call(u, wbd):
    n, cw = u.shape
    l1, l2, f1s, e2, cs = _dft_tables(n)
    scale = 1.0 / math.sqrt(n * FOURIER_GROUP_DIM)
    e2 = jnp.asarray(e2, dtype=BF16)
    cs = jnp.asarray(cs, dtype=BF16)
    if l1 == 1:
        z = u.reshape(1, 1, l2, cw)
        y = _dft_stage2_call(e2, z, cs, wbd, scale=scale, real_only=True)
        return y.reshape(n, cw)
    z = _dft_stage1_call(jnp.asarray(f1s, dtype=BF16), u.reshape(l1, l2 * cw))
    y = _dft_stage2_call(e2, z.reshape(2, l1, l2, cw), cs, wbd, scale=scale, real_only=False)
    return jnp.transpose(y, (1, 0, 2)).reshape(n, cw)


def _store_token_tiles(ref, x):
    n, d = x.shape
    for s in range(d // LANES):
        ref[pl.ds(s, n, stride=SUBLANES), :] = x[:, s * LANES:(s + 1) * LANES]


def _load_token_tiles(ref, n):
    parts = [ref[pl.ds(s, n, stride=SUBLANES), :] for s in range(SUBLANES)]
    return jnp.concatenate(parts, axis=1)


def _out_kernel(mix_ref, x_ref, w_ref, m_ref, g2_ref, wr_ref, br_ref,
                xo_ref, h2_ref, gmat_ref, g4_ref, i4_ref):
    o = jnp.dot(mix_ref[...], w_ref[...], preferred_element_type=F32)
    xn = x_ref[...] + m_ref[0, 2:3, :] * o
    xo_ref[...] = xn
    ms = jnp.mean(xn * xn, axis=-1, keepdims=True)
    h2 = xn * lax.rsqrt(ms + NORM_EPS) * g2_ref[...]
    h2 = h2 * (1.0 + m_ref[0, 4:5, :]) + m_ref[0, 3:4, :]
    _store_token_tiles(h2_ref, h2)
    logits = jnp.dot(h2, wr_ref[...], precision=lax.Precision.HIGHEST,
                     preferred_element_type=F32) + br_ref[...]
    iota = lax.broadcasted_iota(jnp.int32, logits.shape, 1).astype(F32)
    vals, idxs = [], []
    l = logits
    for _ in range(TOP_K):
        mx = jnp.max(l, axis=1, keepdims=True)
        ik = jnp.min(jnp.where(l == mx, iota, float(ROUTER_PAD)), axis=1, keepdims=True)
        vals.append(mx)
        idxs.append(ik)
        l = jnp.where(iota == ik, -jnp.inf, l)
    es = [jnp.exp(v - vals[0]) for v in vals]
    den = es[0] + es[1] + es[2] + es[3]
    gs = [e / den for e in es]
    gmat = jnp.zeros(logits.shape, F32)
    for ik, g in zip(idxs, gs):
        gmat = jnp.where(iota == ik, g, gmat)
    gmat_ref[...] = gmat
    lane4 = lax.broadcasted_iota(jnp.int32, g4_ref.shape, 1)
    g4 = jnp.zeros(g4_ref.shape, F32)
    i4 = jnp.zeros(g4_ref.shape, F32)
    for k in range(TOP_K):
        g4 = jnp.where(lane4 == k, gs[k], g4)
        i4 = jnp.where(lane4 == k, idxs[k], i4)
    g4_ref[...] = g4
    i4_ref[...] = i4.astype(jnp.int32)


def _out_call(mix, xsrc, x_row0, w_out, mods, g2, wr, br, n_ctx_rows):
    n, d = mix.shape
    tm = _pick(n, (256, 128, 64, 32, 16), x_row0, n_ctx_rows)
    xb0 = x_row0 // tm
    nct = n_ctx_rows // tm
    row = lambda i: (i, 0)
    const = lambda i: (0, 0)
    return pl.pallas_call(
        _out_kernel,
        grid=(n // tm,),
        in_specs=[
            pl.BlockSpec((tm, d), row),
            pl.BlockSpec((tm, d), lambda i: (i + xb0, 0)),
            pl.BlockSpec((d, d), const),
            pl.BlockSpec((1, 6, d), lambda i: (jnp.where(i < nct, 1, 0), 0, 0)),
            pl.BlockSpec((1, d), const),
            pl.BlockSpec((d, ROUTER_PAD), const),
            pl.BlockSpec((1, ROUTER_PAD), const),
        ],
        out_specs=[
            pl.BlockSpec((tm, d), row),
            pl.BlockSpec((tm * SUBLANES, LANES), row),
            pl.BlockSpec((tm, ROUTER_PAD), row),
            pl.BlockSpec((tm, TOP_K), row),
            pl.BlockSpec((tm, TOP_K), row),
        ],
        out_shape=[
            jax.ShapeDtypeStruct((n, d), F32),
            jax.ShapeDtypeStruct((n * SUBLANES, LANES), F32),
            jax.ShapeDtypeStruct((n, ROUTER_PAD), F32),
            jax.ShapeDtypeStruct((n, TOP_K), F32),
            jax.ShapeDtypeStruct((n, TOP_K), jnp.int32),
        ],
        compiler_params=_cparams(("arbitrary",)),
    )(mix, xsrc, w_out, mods, g2.reshape(1, d), wr, br)


def _rank_kernel(g_ref, i4_ref, rank_ref, cnt_ref, carry_sc):
    @pl.when(pl.program_id(0) == 0)
    def _():
        carry_sc[...] = jnp.zeros(carry_sc.shape, F32)

    self32 = jnp.where(g_ref[...] > 0.0, 1.0, 0.0)
    sel = self32.astype(BF16)
    tm = sel.shape[0]
    r = lax.broadcasted_iota(jnp.int32, (tm, tm), 0)
    c = lax.broadcasted_iota(jnp.int32, (tm, tm), 1)
    tri = jnp.where(c < r, 1.0, 0.0).astype(BF16)
    excl = jnp.dot(tri, sel, preferred_element_type=F32) + carry_sc[0:1, :]
    lane = lax.broadcasted_iota(jnp.int32, excl.shape, 1)
    lane4 = lax.broadcasted_iota(jnp.int32, rank_ref.shape, 1)
    i4 = i4_ref[...]
    rank4 = jnp.zeros(rank_ref.shape, F32)
    for k in range(TOP_K):
        rk = jnp.sum(jnp.where(lane == i4[:, k:k + 1], excl, 0.0), axis=1, keepdims=True)
        rank4 = jnp.where(lane4 == k, rk, rank4)
    rank_ref[...] = rank4.astype(jnp.int32)
    tot = carry_sc[0:1, :] + jnp.sum(self32, axis=0, keepdims=True)
    carry_sc[...] = jnp.broadcast_to(tot, carry_sc.shape)
    cnt_ref[...] = jnp.broadcast_to(tot, cnt_ref.shape).astype(jnp.int32)


def _rank_call(gmat, i4):
    n, w = gmat.shape
    tm = _pick(n, (256, 128, 64, 32, 16))
    return pl.pallas_call(
        _rank_kernel,
        grid=(n // tm,),
        in_specs=[pl.BlockSpec((tm, w), lambda i: (i, 0)),
                  pl.BlockSpec((tm, TOP_K), lambda i: (i, 0))],
        out_specs=[pl.BlockSpec((tm, TOP_K), lambda i: (i, 0)),
                   pl.BlockSpec((SUBLANES, w), lambda i: (0, 0))],
        out_shape=[jax.ShapeDtypeStruct((n, TOP_K), jnp.int32),
                   jax.ShapeDtypeStruct((SUBLANES, w), jnp.int32)],
        scratch_shapes=[pltpu.VMEM((SUBLANES, w), F32)],
        compiler_params=_cparams(("arbitrary",)),
    )(gmat, i4)


def _dispatch_kernel(dest_ref, h_ref, xs_in_ref, xs_ref, sem, *, tt):
    del xs_in_ref
    base = pl.program_id(0) * (tt * TOP_K)

    def copy(t, k):
        src = pl.multiple_of(t * SUBLANES, SUBLANES)
        dst = pl.multiple_of(dest_ref[base + t * TOP_K + k] * SUBLANES, SUBLANES)
        return pltpu.make_async_copy(h_ref.at[pl.ds(src, SUBLANES)],
                                     xs_ref.at[pl.ds(dst, SUBLANES)], sem)

    def issue(t, c):
        for k in range(TOP_K):
            copy(t, k).start(priority=k % DMA_PRIORITIES)
        return c

    def drain(t, c):
        for k in range(TOP_K):
            copy(t, k).wait()
        return c

    lax.fori_loop(0, tt, issue, 0, unroll=DMA_LOOP_UNROLL)
    lax.fori_loop(0, tt, drain, 0, unroll=DMA_LOOP_UNROLL)


def _dispatch_call(dest_flat, h2t, n_slots):
    n = h2t.shape[0] // SUBLANES
    tt = _pick(n, (256, 128, 64, 32, 16))
    xs0 = jnp.zeros((n_slots * SUBLANES, LANES), F32)
    return pl.pallas_call(
        functools.partial(_dispatch_kernel, tt=tt),
        grid_spec=pltpu.PrefetchScalarGridSpec(
            num_scalar_prefetch=1,
            grid=(n // tt,),
            in_specs=[pl.BlockSpec((tt * SUBLANES, LANES), lambda i, dref: (i, 0)),
                      pl.BlockSpec(memory_space=pl.ANY)],
            out_specs=pl.BlockSpec(memory_space=pl.ANY),
            scratch_shapes=[pltpu.SemaphoreType.DMA(())],
        ),
        out_shape=jax.ShapeDtypeStruct(xs0.shape, F32),
        input_output_aliases={2: 0},
        compiler_params=_cparams(("arbitrary",)),
    )(dest_flat, h2t, xs0)


def _expert_kernel(be_ref, nb_ref, x_ref, wgu_ref, bgu_ref, wd_ref, bd_ref, y_ref, wgu_sc, wd_sc):
    b = pl.program_id(0)
    active = b < nb_ref[0]

    @pl.when(jnp.logical_and(active, jnp.logical_or(b == 0, be_ref[b] != be_ref[jnp.maximum(b - 1, 0)])))
    def _():
        wgu_sc[...] = wgu_ref[0, 0].astype(BF16)
        wd_sc[...] = wd_ref[0, 0].astype(BF16)

    @pl.when(active)
    def _():
        x = _load_token_tiles(x_ref, MOE_BLOCK).astype(BF16)
        gu = jnp.dot(x, wgu_sc[...], preferred_element_type=F32) + bgu_ref[0]
        glu = jnp.minimum(gu[:, :EXPERT_FF], SWIGLU_LIMIT)
        lin = jnp.clip(gu[:, EXPERT_FF:], -SWIGLU_LIMIT, SWIGLU_LIMIT)
        act = glu * jax.nn.sigmoid(SWIGLU_ALPHA * glu) * (lin + 1.0)
        y = jnp.dot(act.astype(BF16), wd_sc[...], preferred_element_type=F32) + bd_ref[0]
        _store_token_tiles(y_ref, y)

    @pl.when(pl.program_id(0) >= nb_ref[0])
    def _():
        y_ref[...] = jnp.zeros(y_ref.shape, F32)


def _expert_call(blk_e, nb_used, xs, layer, wgu, bgu, wd, bd):
    nb = xs.shape[0] // (MOE_BLOCK * SUBLANES)
    _, ne, d, ff2 = wgu.shape
    blk = lambda b, be, nbu: (jnp.minimum(b, nbu[0] - 1), 0)
    wsel = lambda b, be, nbu: (layer, be[jnp.minimum(b, nbu[0] - 1)], 0, 0)
    bsel = lambda b, be, nbu: (be[jnp.minimum(b, nbu[0] - 1)], 0, 0)
    return pl.pallas_call(
        _expert_kernel,
        grid_spec=pltpu.PrefetchScalarGridSpec(
            num_scalar_prefetch=2,
            grid=(nb,),
            in_specs=[
                pl.BlockSpec((MOE_BLOCK * SUBLANES, LANES), blk),
                pl.BlockSpec((1, 1, d, ff2), wsel),
                pl.BlockSpec((1, 1, ff2), bsel),
                pl.BlockSpec((1, 1, ff2 // 2, d), wsel),
                pl.BlockSpec((1, 1, d), bsel),
            ],
            out_specs=pl.BlockSpec((MOE_BLOCK * SUBLANES, LANES), lambda b, be, nbu: (b, 0)),
            scratch_shapes=[pltpu.VMEM((d, ff2), BF16), pltpu.VMEM((ff2 // 2, d), BF16)],
        ),
        out_shape=jax.ShapeDtypeStruct(xs.shape, F32),
        compiler_params=_cparams(("arbitrary",)),
    )(blk_e, nb_used, xs, wgu, bgu.reshape(ne, 1, ff2), wd, bd.reshape(ne, 1, d))


def _combine_kernel(dest_ref, ys_ref, x_ref, g4_ref, m_ref, gf_ref, o_ref, buf, sem, *, tm, final):
    base = pl.program_id(0) * tm * TOP_K

    def copy(t, k):
        src = pl.multiple_of(dest_ref[base + t * TOP_K + k] * SUBLANES, SUBLANES)
        dst = pl.multiple_of(t * SUBLANES, SUBLANES)
        return pltpu.make_async_copy(ys_ref.at[pl.ds(src, SUBLANES)],
                                     buf.at[k, pl.ds(dst, SUBLANES)], sem)

    def issue(t, c):
        for k in range(TOP_K):
            copy(t, k).start(priority=k % DMA_PRIORITIES)
        return c

    def drain(t, c):
        for k in range(TOP_K):
            copy(t, k).wait()
        return c

    lax.fori_loop(0, tm, issue, 0, unroll=DMA_LOOP_UNROLL)
    lax.fori_loop(0, tm, drain, 0, unroll=DMA_LOOP_UNROLL)

    g4 = g4_ref[...]
    y = _load_token_tiles(buf.at[0], tm) * g4[:, 0:1]
    for k in range(1, TOP_K):
        y = y + _load_token_tiles(buf.at[k], tm) * g4[:, k:k + 1]
    xo = x_ref[...] + m_ref[0, 5:6, :] * y
    if final:
        ms = jnp.mean(xo * xo, axis=-1, keepdims=True)
        xo = xo * lax.rsqrt(ms + NORM_EPS) * gf_ref[...]
    o_ref[...] = xo


def _combine_call(dest_flat, ys, x, g4, mods, g_final, n_ctx_rows, final):
    n, d = x.shape
    tm = _pick(n, (256, 128, 64, 32, 16), n_ctx_rows)
    nct = n_ctx_rows // tm
    row = lambda i, dref: (i, 0)
    return pl.pallas_call(
        functools.partial(_combine_kernel, tm=tm, final=final),
        grid_spec=pltpu.PrefetchScalarGridSpec(
            num_scalar_prefetch=1,
            grid=(n // tm,),
            in_specs=[
                pl.BlockSpec(memory_space=pl.ANY),
                pl.BlockSpec((tm, d), row),
                pl.BlockSpec((tm, TOP_K), row),
                pl.BlockSpec((1, 6, d), lambda i, dref: (jnp.where(i < nct, 1, 0), 0, 0)),
                pl.BlockSpec((1, d), lambda i, dref: (0, 0)),
            ],
            out_specs=pl.BlockSpec((tm, d), row),
            scratch_shapes=[pltpu.VMEM((TOP_K, tm * SUBLANES, LANES), F32),
                            pltpu.SemaphoreType.DMA(())],
        ),
        out_shape=jax.ShapeDtypeStruct((n, d), F32),
        compiler_params=_cparams(("arbitrary",)),
    )(dest_flat, ys, x, g4, mods, g_final.reshape(1, d))


def _moe(x_mid, h2t, gmat, g4, i4, mods, n_ctx_rows, layer, wgu, bgu, wd, bd, g_final, final):
    n, d = x_mid.shape
    assert d == SUBLANES * LANES
    rank4, cnt = _rank_call(gmat, i4)
    counts = cnt[0, :N_EXPERTS]
    padded = (counts + MOE_BLOCK - 1) // MOE_BLOCK * MOE_BLOCK
    ends = jnp.cumsum(padded)
    starts = ends - padded
    start4 = jnp.sum(jnp.where(i4[..., None] == jnp.arange(N_EXPERTS, dtype=jnp.int32), starts, 0),
                     axis=-1)
    dest_flat = (start4 + rank4).reshape(-1).astype(jnp.int32)
    n_slots = (n * TOP_K + MOE_BLOCK - 1) // MOE_BLOCK * MOE_BLOCK + N_EXPERTS * MOE_BLOCK
    nb = n_slots // MOE_BLOCK
    blk_start = jnp.arange(nb, dtype=jnp.int32) * MOE_BLOCK
    blk_e = jnp.minimum(jnp.sum((ends[None, :] <= blk_start[:, None]).astype(jnp.int32), axis=1),
                        N_EXPERTS - 1).astype(jnp.int32)
    nb_used = (ends[-1] // MOE_BLOCK).astype(jnp.int32).reshape(1)
    xs = _dispatch_call(dest_flat, h2t, n_slots)
    ys = _expert_call(blk_e, nb_used, xs, layer, wgu, bgu, wd, bd)
    return _combine_call(dest_flat, ys, x_mid, g4, mods, g_final, n_ctx_rows, final)


def _heads(a, nh):
    r = a.shape[0]
    return jnp.transpose(a.reshape(r, nh, HEAD_DIM), (1, 0, 2))


def _unheads(o):
    n = o.shape[-1]
    o = o.reshape(-1, HEAD_DIM, n)
    return jnp.transpose(o, (2, 0, 1)).reshape(n, -1)


def kernel(x, c, ctx, c_ctx, w_mod, b_mod, g_norm1, g_norm2, w_in, w_out, w_fnet, lambda_q1, lambda_k1, lambda_q2, lambda_k2, g_subln, g_qnorm, g_knorm, w_router, b_router, w_gate_up, b_gate_up, w_down, b_down, g_final):
    bsz, seq, d = x.shape
    assert bsz == 1
    n_ctx = ctx.shape[1]
    depth = w_mod.shape[0]
    r = n_ctx + seq

    cc = jnp.zeros((SUBLANES, d), F32).at[0].set(c[0]).at[1].set(c_ctx)
    mod_all = _mod_call(cc, w_mod, b_mod)
    tabs = _rope_tables(n_ctx, seq)
    bdq = _block_diag_mean(GQA_WIDTH, HEAD_DIM)
    bdk = _block_diag_mean(GQA_KV_WIDTH, HEAD_DIM)

    xall = jnp.concatenate([ctx[0], x[0]], axis=0)
    out = None
    for l in range(depth):
        need_ctx = l < depth - 1
        lam_init = 0.8 - 0.6 * math.exp(-0.3 * l)
        mods = mod_all[l, :2].reshape(2, 6, d)
        wl = w_in[l]
        w_ext = jnp.concatenate([
            wl,
            _swap_halves_cols(wl[:, _C_DQ:_C_DK], DIFF_QK_DIM),
            _swap_halves_cols(wl[:, _C_DK:_C_DV], DIFF_QK_DIM),
            _swap_halves_cols(wl[:, _C_GQ:_C_GK], HEAD_DIM),
            _swap_halves_cols(wl[:, _C_GK:_C_GV], HEAD_DIM),
        ], axis=1).astype(BF16)
        gq = g_qnorm[l]
        gk = g_knorm[l]
        gq_sw = jnp.concatenate([gq[HEAD_DIM // 2:], gq[:HEAD_DIM // 2]])
        gk_sw = jnp.concatenate([gk[HEAD_DIM // 2:], gk[:HEAD_DIM // 2]])
        gq2 = jnp.stack([jnp.tile(gq, N_GQA_Q_HEADS), jnp.tile(gq_sw, N_GQA_Q_HEADS)])
        gk2 = jnp.stack([jnp.tile(gk, N_GQA_KV_HEADS), jnp.tile(gk_sw, N_GQA_KV_HEADS)])

        f, dq, dk, dv, gqo, gko, gvo = _in_call(xall, g_norm1[l], mods, w_ext, tabs, gq2, gk2,
                                                 bdq, bdk, n_ctx)

        dqh = jnp.transpose(dq).reshape(N_DIFF_HEADS, HEAD_DIM, r)
        dkh = _heads(dk, N_DIFF_HEADS)
        dvh = _heads(dv, N_DIFF_HEADS)
        gqh = jnp.transpose(gqo).reshape(N_GQA_KV_HEADS, GQA_GROUP, HEAD_DIM, r)
        gkh = _heads(gko, N_GQA_KV_HEADS)
        gvh = _heads(gvo, N_GQA_KV_HEADS)
        lamp = jnp.stack([lambda_q1[l], lambda_k1[l], lambda_q2[l], lambda_k2[l]]).astype(F32)
        gsub = g_subln[l].reshape(1, HEAD_DIM)
        wbd = jnp.zeros((FOURIER_WIDTH, FOURIER_WIDTH), F32)
        for g in range(N_FOURIER_GROUPS):
            sl = slice(g * FOURIER_GROUP_DIM, (g + 1) * FOURIER_GROUP_DIM)
            wbd = wbd.at[sl, sl].set(w_fnet[l, g])
        wbd = wbd.astype(BF16)

        def mixers(q_row0, n_q, n_k):
            d_o = _attn_call(dqh, dkh, dvh, diff=True, q_row0=q_row0, n_q=n_q, n_k=n_k,
                             lamp=lamp, gsub=gsub, lam_scale=1.0 - lam_init)
            g_o = _attn_call(gqh, gkh, gvh, diff=False, q_row0=q_row0, n_q=n_q, n_k=n_k)
            f_o = _fnet_call(f[q_row0:q_row0 + n_q], wbd)
            return jnp.concatenate([f_o, _unheads(d_o), _unheads(g_o)], axis=1)

        mix = mixers(n_ctx, seq, r)
        if need_ctx:
            mix = jnp.concatenate([mixers(0, n_ctx, n_ctx), mix], axis=0)
            x_row0, nct = 0, n_ctx
        else:
            x_row0, nct = n_ctx, 0

        wr = jnp.zeros((d, ROUTER_PAD), F32).at[:, :N_EXPERTS].set(w_router[l])
        br = jnp.full((1, ROUTER_PAD), NEG_BIG, F32).at[0, :N_EXPERTS].set(b_router[l])
        x_mid, h2, gmat, g4, i4 = _out_call(mix, xall, x_row0, w_out[l].astype(BF16), mods,
                                            g_norm2[l], wr, br, nct)
        xall = _moe(x_mid, h2, gmat, g4, i4, mods, nct,
                    l, w_gate_up, b_gate_up[l], w_down, b_down[l],
                    g_final, final=not need_ctx)
    return xall.reshape(1, seq, d)
```
